```python
import functools
import jax, jax.numpy as jnp
from jax import lax
import numpy as np

D_MODEL = 1024
BATCH = 4
SEQ = 4096
DEPTH = 4
DEC_BATCH = 32
DEC_SEQ = 8
PAST_LEN = 8192
PAGE_SIZE = 128

HEAD_DIM = 64
RWKV_WIDTH = D_MODEL // 2
FOX_WIDTH = D_MODEL - RWKV_WIDTH
N_RWKV_HEADS = RWKV_WIDTH // HEAD_DIM
N_FOX_HEADS = FOX_WIDTH // HEAD_DIM
LORA_DECAY = 64
LORA_ICLR = 64
LORA_GATE = 128
RWKV_IN = 3 * RWKV_WIDTH + LORA_DECAY + LORA_ICLR + LORA_GATE
FOX_IN = 3 * FOX_WIDTH + N_FOX_HEADS
IN_WIDTH = RWKV_IN + FOX_IN
D_FF = 4 * D_MODEL
Q_BLOCK = 128
NORM_EPS = 1e-6
GN_EPS = 64e-5

kernel_name = 'hymba_rwkv7_fox_decode_step'


def rmsnorm(x, w):
    xf = x.astype(jnp.float32)
    y = xf * lax.rsqrt(jnp.mean(xf * xf, axis=-1, keepdims=True) + NORM_EPS)
    return (y * w.astype(jnp.float32)).astype(x.dtype)


def split_cols(z, sizes):
    idx = [int(i) for i in np.cumsum(sizes)[:-1]]
    return jnp.split(z, idx, axis=-1)


def wkv7_step(S, inp):
    r, w, k, v, kk, a = inp
    sa = jnp.einsum('bhvk,bhk->bhv', S, -kk)
    S = S * w[:, :, None, :] + sa[..., None] * (kk * a)[:, :, None, :] + v[..., None] * k[:, :, None, :]
    y = jnp.einsum('bhvk,bhk->bhv', S, r)
    return S, y


def rwkv7_mixer(z, shift_prev, wkv0, p):
    f32 = jnp.float32
    B, T, _ = z.shape
    z_prev = jnp.concatenate([shift_prev[:, None, :].astype(z.dtype), z[:, :-1]], axis=1)
    zs = z + (z_prev - z) * p['mu']
    r, k, v, wd, ad, gd = split_cols(zs, [RWKV_WIDTH] * 3 + [LORA_DECAY, LORA_ICLR, LORA_GATE])
    w_log = -jax.nn.softplus(-(p['w0'] + jnp.tanh(wd) @ p['w2']).astype(f32)) - 0.5
    decay = jnp.exp(-jnp.exp(w_log))
    a = jax.nn.sigmoid((p['a0'] + ad @ p['a2']).astype(f32))
    g = jax.nn.sigmoid(gd) @ p['g2']
    heads = lambda t: t.astype(f32).reshape(B, T, N_RWKV_HEADS, HEAD_DIM)
    kk = heads(k * p['k_k'])
    kk = kk * lax.rsqrt(jnp.maximum(jnp.sum(kk * kk, axis=-1, keepdims=True), 1e-24))
    k_mod = k.astype(f32) * (1.0 + (a - 1.0) * p['k_a'].astype(f32))
    rh, kh, vh, wh, ah = heads(r), heads(k_mod), heads(v), heads(decay), heads(a)
    xs = tuple(jnp.moveaxis(t, 1, 0) for t in (rh, wh, kh, vh, kk, ah))
    S, ys = lax.scan(wkv7_step, wkv0.astype(f32), xs)
    y = jnp.moveaxis(ys, 0, 1)
    mean = jnp.mean(y, axis=-1, keepdims=True)
    var = jnp.mean(jnp.square(y - mean), axis=-1, keepdims=True)
    ln_w = p['ln_w'].astype(f32).reshape(N_RWKV_HEADS, HEAD_DIM)
    ln_b = p['ln_b'].astype(f32).reshape(N_RWKV_HEADS, HEAD_DIM)
    yn = (y - mean) * lax.rsqrt(var + GN_EPS) * ln_w + ln_b
    bonus = jnp.sum(rh * kh * p['r_k'].astype(f32), axis=-1, keepdims=True) * vh
    out = ((yn + bonus).reshape(B, T, RWKV_WIDTH) * g.astype(f32)).astype(z.dtype)
    return out, S.astype(wkv0.dtype), z[:, -1]


def forget_attention_block(q, c_q, q_pos, k, v, c_k, k_pos):
    f32 = jnp.float32
    s = jnp.einsum('bqhd,bkhd->bhqk', q.astype(f32), k) * (HEAD_DIM ** -0.5)
    bias = jnp.transpose(c_q, (0, 2, 1))[..., :, None] - jnp.transpose(c_k, (0, 2, 1))[..., None, :]
    causal = k_pos[None, :] <= q_pos[:, None]
    logits = jnp.where(causal, s + bias, -jnp.inf)
    probs = jax.nn.softmax(logits, axis=-1)
    return jnp.einsum('bhqk,bkhd->bqhd', probs, v)


def fox_prompt(q, k, v, logf):
    f32 = jnp.float32
    B, T, H, D = q.shape
    c = jnp.cumsum(logf, axis=1)
    nb = T // Q_BLOCK
    qb = jnp.moveaxis(q.reshape(B, nb, Q_BLOCK, H, D), 1, 0)
    cb = jnp.moveaxis(c.reshape(B, nb, Q_BLOCK, H), 1, 0)
    starts = jnp.arange(nb, dtype=jnp.int32) * Q_BLOCK
    k_pos = jnp.arange(T, dtype=jnp.int32)
    kf, vf = k.astype(f32), v.astype(f32)

    def one_block(args):
        q_i, c_i, s_i = args
        return forget_attention_block(q_i, c_i, s_i + jnp.arange(Q_BLOCK, dtype=jnp.int32), kf, vf, c, k_pos)

    o = lax.map(one_block, (qb, cb, starts))
    return jnp.moveaxis(o, 0, 1).reshape(B, T, H, D)


def fox_sample(q, k, v, logf, k_past, v_past, logf_past):
    f32 = jnp.float32
    T = q.shape[1]
    P = k_past.shape[1]
    c_past = jnp.cumsum(logf_past.astype(f32), axis=1)
    c_new = c_past[:, -1:] + jnp.cumsum(logf, axis=1)
    k_all = jnp.concatenate([k_past.astype(f32), k.astype(f32)], axis=1)
    v_all = jnp.concatenate([v_past.astype(f32), v.astype(f32)], axis=1)
    c_all = jnp.concatenate([c_past, c_new], axis=1)
    k_pos = jnp.arange(P + T, dtype=jnp.int32)
    q_pos = P + jnp.arange(T, dtype=jnp.int32)
    return forget_attention_block(q, c_new, q_pos, k_all, v_all, c_all, k_pos)


def trunk_layer(x, wkv0, shift0, attend, p):
    f32 = jnp.float32
    B, T, _ = x.shape
    h = rmsnorm(x, p['norm_mix'])
    z = h @ p['w_in']
    z_rwkv, z_fox = z[..., :RWKV_IN], z[..., RWKV_IN:]
    y_rwkv, wkv1, shift1 = rwkv7_mixer(z_rwkv, shift0, wkv0, p)
    q, k, v, f_logit = split_cols(z_fox, [FOX_WIDTH] * 3 + [N_FOX_HEADS])
    qh = q.reshape(B, T, N_FOX_HEADS, HEAD_DIM)
    kh = k.reshape(B, T, N_FOX_HEADS, HEAD_DIM)
    vh = v.reshape(B, T, N_FOX_HEADS, HEAD_DIM)
    logf = jax.nn.log_sigmoid((f_logit + p['b_f']).astype(f32))
    o = attend(qh, kh, vh, logf)
    o = o * lax.rsqrt(jnp.mean(o * o, axis=-1, keepdims=True) + NORM_EPS) * p['fox_norm'].astype(f32).reshape(N_FOX_HEADS, HEAD_DIM)
    mix = jnp.concatenate([y_rwkv, o.reshape(B, T, FOX_WIDTH).astype(x.dtype)], axis=-1)
    x = x + mix @ p['w_out']
    h2 = rmsnorm(x, p['norm_ffn'])
    x = x + jnp.square(jax.nn.relu(h2 @ p['w_up'])) @ p['w_down']
    return x, (kh, vh, logf, wkv1, shift1)


def setup_inputs(seed: int = 0) -> dict:
    key = jax.random.key(seed)
    ks = iter(jax.random.split(key, 40))
    f32 = jnp.float32

    def normal(shape, scale=1.0):
        return scale * jax.random.normal(next(ks), shape, f32)

    n_pages = PAST_LEN // PAGE_SIZE
    n_used = DEC_BATCH * n_pages
    n_pool = (5 * n_used + 3) // 4
    x_prompt = normal((BATCH, SEQ, D_MODEL))
    x_sample = normal((DEC_BATCH, DEC_SEQ, D_MODEL))
    cache_k = normal((DEPTH, n_pool, PAGE_SIZE, N_FOX_HEADS, HEAD_DIM))
    cache_v = normal((DEPTH, n_pool, PAGE_SIZE, N_FOX_HEADS, HEAD_DIM))
    cache_logf = jax.nn.log_sigmoid(normal((DEPTH, n_pool, PAGE_SIZE, N_FOX_HEADS)) + 2.0)
    state_wkv = normal((DEPTH, DEC_BATCH, N_RWKV_HEADS, HEAD_DIM, HEAD_DIM), 0.5)
    state_shift = normal((DEPTH, DEC_BATCH, RWKV_IN))
    page_table = jax.random.permutation(next(ks), n_pool)[:n_used].reshape(DEC_BATCH, n_pages).astype(jnp.int32)
    norm_mix = 1.0 + normal((DEPTH, D_MODEL), 0.05)
    w_in = normal((DEPTH, D_MODEL, IN_WIDTH), D_MODEL ** -0.5)
    mu = jax.random.uniform(next(ks), (DEPTH, RWKV_IN), f32)
    w0 = -2.0 + normal((DEPTH, RWKV_WIDTH), 0.5)
    w2 = normal((DEPTH, LORA_DECAY, RWKV_WIDTH), 0.5 * LORA_DECAY ** -0.5)
    a0 = normal((DEPTH, RWKV_WIDTH), 0.1)
    a2 = normal((DEPTH, LORA_ICLR, RWKV_WIDTH), LORA_ICLR ** -0.5)
    g2 = normal((DEPTH, LORA_GATE, RWKV_WIDTH), LORA_GATE ** -0.5)
    k_k = 0.85 + normal((DEPTH, RWKV_WIDTH), 0.05)
    k_a = 1.0 + normal((DEPTH, RWKV_WIDTH), 0.05)
    r_k = normal((DEPTH, N_RWKV_HEADS, HEAD_DIM), 0.1)
    ln_w = 1.0 + normal((DEPTH, RWKV_WIDTH), 0.05)
    ln_b = normal((DEPTH, RWKV_WIDTH), 0.01)
    b_f = 1.0 + normal((DEPTH, N_FOX_HEADS), 0.1)
    fox_norm = 1.0 + normal((DEPTH, FOX_WIDTH), 0.05)
    w_out = normal((DEPTH, D_MODEL, D_MODEL), D_MODEL ** -0.5)
    norm_ffn = 1.0 + normal((DEPTH, D_MODEL), 0.05)
    w_up = normal((DEPTH, D_MODEL, D_FF), D_MODEL ** -0.5)
    w_down = normal((DEPTH, D_FF, D_MODEL), D_FF ** -0.5)
    norm_final = 1.0 + normal((D_MODEL,), 0.05)
    return {'x_prompt': x_prompt, 'x_sample': x_sample, 'cache_k': cache_k, 'cache_v': cache_v,
            'cache_logf': cache_logf, 'state_wkv': state_wkv, 'state_shift': state_shift,
            'page_table': page_table, 'norm_mix': norm_mix, 'w_in': w_in, 'mu': mu, 'w0': w0,
            'w2': w2, 'a0': a0, 'a2': a2, 'g2': g2, 'k_k': k_k, 'k_a': k_a, 'r_k': r_k,
            'ln_w': ln_w, 'ln_b': ln_b, 'b_f': b_f, 'fox_norm': fox_norm, 'w_out': w_out,
            'norm_ffn': norm_ffn, 'w_up': w_up, 'w_down': w_down, 'norm_final': norm_final}


def reference(x_prompt, x_sample, cache_k, cache_v, cache_logf, state_wkv, state_shift, page_table,
              norm_mix, w_in, mu, w0, w2, a0, a2, g2, k_k, k_a, r_k, ln_w, ln_b, b_f, fox_norm,
              w_out, norm_ffn, w_up, w_down, norm_final):
    dec_b = x_sample.shape[0]
    n_past = page_table.shape[1] * PAGE_SIZE
    wkv_p0 = jnp.zeros((x_prompt.shape[0], N_RWKV_HEADS, HEAD_DIM, HEAD_DIM), jnp.float32)
    shift_p0 = jnp.zeros((x_prompt.shape[0], RWKV_IN), x_prompt.dtype)
    xp, xs = x_prompt, x_sample
    outs_p, outs_s = [], []
    for l in range(DEPTH):
        p = dict(norm_mix=norm_mix[l], w_in=w_in[l], mu=mu[l], w0=w0[l], w2=w2[l], a0=a0[l],
                 a2=a2[l], g2=g2[l], k_k=k_k[l], k_a=k_a[l], r_k=r_k[l], ln_w=ln_w[l],
                 ln_b=ln_b[l], b_f=b_f[l], fox_norm=fox_norm[l], w_out=w_out[l],
                 norm_ffn=norm_ffn[l], w_up=w_up[l], w_down=w_down[l])
        xp, st_p = trunk_layer(xp, wkv_p0, shift_p0, fox_prompt, p)
        k_past = cache_k[l, page_table].reshape(dec_b, n_past, N_FOX_HEADS, HEAD_DIM)
        v_past = cache_v[l, page_table].reshape(dec_b, n_past, N_FOX_HEADS, HEAD_DIM)
        lf_past = cache_logf[l, page_table].reshape(dec_b, n_past, N_FOX_HEADS)
        attend = functools.partial(fox_sample, k_past=k_past, v_past=v_past, logf_past=lf_past)
        xs, st_s = trunk_layer(xs, state_wkv[l], state_shift[l], attend, p)
        outs_p.append(st_p)
        outs_s.append(st_s)
    stack = lambda outs, i: jnp.stack([o[i] for o in outs])
    y_prompt = rmsnorm(xp, norm_final)
    y_sample = rmsnorm(xs, norm_final)
    return (y_prompt, y_sample,
            stack(outs_p, 0), stack(outs_p, 1), stack(outs_p, 2), stack(outs_p, 3), stack(outs_p, 4),
            stack(outs_s, 0), stack(outs_s, 1), stack(outs_s, 2), stack(outs_s, 3), stack(outs_s, 4))
```

```python
import functools

import jax
import jax.numpy as jnp
from jax import lax
from jax.experimental import pallas as pl
from jax.experimental.pallas import tpu as pltpu

F32 = jnp.float32
BF16 = jnp.bfloat16

HEAD_DIM = 64
PAIR = 2 * HEAD_DIM
N_PAIRS = 4
MIX_WIDTH = 512
RWKV_IN = 1792
LORA_WA = 128
NORM_EPS = 1e-6
GN_EPS = 64e-5
CHUNK = 64
NEG_BIG = -1e30
VMEM_LIMIT = 56 * 1024 * 1024


def _dot(a, b):
    return jnp.dot(a, b, preferred_element_type=F32)


def _dot_nt(a, b):
    return lax.dot_general(a, b, (((1,), (1,)), ((), ())), preferred_element_type=F32)


def _dot_tn(a, b):
    return lax.dot_general(a, b, (((0,), (0,)), ((), ())), preferred_element_type=F32)


def _split2(x):
    hi = x.astype(BF16)
    lo = (x - hi.astype(F32)).astype(BF16)
    return hi, lo


def _split3(x):
    hi = x.astype(BF16)
    r1 = x - hi.astype(F32)
    mid = r1.astype(BF16)
    lo = (r1 - mid.astype(F32)).astype(BF16)
    return hi, mid, lo


def _dot_const_l(c, x):
    hi, mid, lo = _split3(x)
    return _dot(c, hi) + _dot(c, mid) + _dot(c, lo)


def _dot_const_r(x, c):
    hi, mid, lo = _split3(x)
    return _dot(hi, c) + _dot(mid, c) + _dot(lo, c)


def _mm_hi(a, b):
    ah, al = _split2(a)
    bh, bl = _split2(b)
    return _dot(ah, bh) + _dot(ah, bl) + _dot(al, bh)


def _rms(x, w):
    return x * lax.rsqrt(jnp.mean(x * x, axis=-1, keepdims=True) + NORM_EPS) * w


def _log_sigmoid(x):
    return jnp.minimum(x, 0.0) - jnp.log1p(jnp.exp(-jnp.abs(x)))


def _softplus(x):
    return jnp.maximum(x, 0.0) + jnp.log1p(jnp.exp(-jnp.abs(x)))


def _sigmoid(x):
    return 1.0 / (1.0 + jnp.exp(-x))


def _iota2(shape, dim):
    return lax.broadcasted_iota(jnp.int32, shape, dim)


def _seg_ones(n, seg):
    r = _iota2((n, n), 0) // seg
    c = _iota2((n, n), 1) // seg
    return (r == c).astype(BF16)


def _seg_sum(x, ones):
    hi, lo = _split2(x)
    return _dot(hi, ones) + _dot(lo, ones)


def _const_spec(shape):
    nd = len(shape)
    return pl.BlockSpec(shape, lambda *_: (0,) * nd)


def _inproj_kernel(x_ref, nw_ref, wr_ref, wqkv_ref, wf_ref, wft_ref, bf_ref, bft_ref,
                   zr_ref, k_ref, v_ref, qb_ref, kb_ref, vb_ref, lf_ref, lft_ref):
    x = x_ref[...]
    h = _rms(x, nw_ref[...]).astype(BF16)
    zr_ref[...] = _dot(h, wr_ref[...])
    qkv = _dot(h, wqkv_ref[...])
    q = qkv[:, :MIX_WIDTH]
    k = qkv[:, MIX_WIDTH:2 * MIX_WIDTH]
    v = qkv[:, 2 * MIX_WIDTH:]
    k_ref[...] = k
    v_ref[...] = v
    qb_ref[...] = (q * (HEAD_DIM ** -0.5)).astype(BF16)
    kb_ref[...] = k.astype(BF16)
    vb_ref[...] = v.astype(BF16)
    lf_ref[...] = _log_sigmoid(_dot(h, wf_ref[...]) + bf_ref[...])
    lft = _log_sigmoid(_dot_nt(wft_ref[...], h) + bft_ref[...])
    lft_ref[...] = lft[:8]


def _inproj(x, nw, wr, wqkv, wf, wft, bfp, bft, tm):
    n, d = x.shape
    grid = (n // tm,)
    row = lambda w: pl.BlockSpec((tm, w), lambda i: (i, 0))
    out_shape = (
        jax.ShapeDtypeStruct((n, RWKV_IN), F32),
        jax.ShapeDtypeStruct((n, MIX_WIDTH), F32),
        jax.ShapeDtypeStruct((n, MIX_WIDTH), F32),
        jax.ShapeDtypeStruct((n, MIX_WIDTH), BF16),
        jax.ShapeDtypeStruct((n, MIX_WIDTH), BF16),
        jax.ShapeDtypeStruct((n, MIX_WIDTH), BF16),
        jax.ShapeDtypeStruct((n, 128), F32),
        jax.ShapeDtypeStruct((8, n), F32),
    )
    return pl.pallas_call(
        _inproj_kernel,
        grid=grid,
        in_specs=[row(d), _const_spec(nw.shape), _const_spec(wr.shape), _const_spec(wqkv.shape),
                  _const_spec(wf.shape), _const_spec(wft.shape), _const_spec(bfp.shape),
                  _const_spec(bft.shape)],
        out_specs=(row(RWKV_IN), row(MIX_WIDTH), row(MIX_WIDTH), row(MIX_WIDTH), row(MIX_WIDTH),
                   row(MIX_WIDTH), row(128), pl.BlockSpec((8, tm), lambda i: (0, i))),
        out_shape=out_shape,
        compiler_params=pltpu.CompilerParams(dimension_semantics=("arbitrary",),
                                             vmem_limit_bytes=VMEM_LIMIT),
        name="inproj",
    )(x, nw, wr, wqkv, wf, wft, bfp, bft)


def _outmlp_kernel(x_ref, yr_ref, of_ref, wo_ref, nf_ref, wu_ref, wd_ref, nfin_ref, o_ref,
                   *, ff_tile, final):
    wo = wo_ref
    x = x_ref[...] + _dot(yr_ref[...], wo[:MIX_WIDTH, :]) + _dot(of_ref[...], wo[MIX_WIDTH:, :])
    h2 = _rms(x, nf_ref[...]).astype(BF16)
    mlp = None
    d_ff = wu_ref.shape[1]
    for j in range(d_ff // ff_tile):
        u = _dot(h2, wu_ref[:, j * ff_tile:(j + 1) * ff_tile])
        u = jnp.square(jnp.maximum(u, 0.0)).astype(BF16)
        t = _dot(u, wd_ref[j * ff_tile:(j + 1) * ff_tile, :])
        mlp = t if mlp is None else mlp + t
    acc = x + mlp
    if final:
        acc = _rms(acc, nfin_ref[...])
    o_ref[...] = acc


def _outmlp(x, yr, of, wo, nf, wu, wd, nfin, tm, final):
    n, d = x.shape
    grid = (n // tm,)
    row = lambda w: pl.BlockSpec((tm, w), lambda i: (i, 0))
    kern = functools.partial(_outmlp_kernel, ff_tile=1024, final=final)
    return pl.pallas_call(
        kern,
        grid=grid,
        in_specs=[row(d), row(MIX_WIDTH), row(MIX_WIDTH), _const_spec(wo.shape), _const_spec(nf.shape),
                  _const_spec(wu.shape), _const_spec(wd.shape), _const_spec(nfin.shape)],
        out_specs=row(d),
        out_shape=jax.ShapeDtypeStruct((n, d), F32),
        compiler_params=pltpu.CompilerParams(dimension_semantics=("arbitrary",),
                                             vmem_limit_bytes=VMEM_LIMIT),
        name="outmlp",
    )(x, yr, of, wo, nf, wu, wd, nfin)


def _unit_lower_inverse(n_mat, c):
    r = _iota2((c, c), 0)
    col = _iota2((c, c), 1)
    eye = (r == col).astype(F32)
    same8 = (r // 8) == (col // 8)
    n8 = jnp.where(same8, n_mat, 0.0)
    n8_2 = _mm_hi(n8, n8)
    n8_4 = _mm_hi(n8_2, n8_2)
    t = eye + n8
    t = t + _mm_hi(t, n8_2)
    t = t + _mm_hi(t, n8_4)
    size = 8
    while size < c // 2:
        same_lo = (r // size) == (col // size)
        same_hi = (r // (2 * size)) == (col // (2 * size))
        n_off = jnp.where(jnp.logical_and(same_hi, jnp.logical_not(same_lo)), n_mat, 0.0)
        t = t + _mm_hi(t, _mm_hi(n_off, t))
        size *= 2
    return t


def _rwkv_kernel(z_ref, sh0_ref, s0_ref, mu_ref, w0_ref, a0_ref, kk_ref, ka_ref, rk_ref,
                 lnw_ref, lnb_ref, w2_ref, a2_ref, g2_ref,
                 y_ref, sout_ref, carry_ref, s_ref, *, t_valid):
    c = pl.program_id(1)
    n_chunks = pl.num_programs(1)
    C = z_ref.shape[1]
    C2 = 2 * C

    @pl.when(c == 0)
    def _():
        carry_ref[...] = sh0_ref[0]
        s_ref[...] = s0_ref[0]

    z = z_ref[0]
    rows = _iota2((C, 1), 0)
    z_prev = jnp.where(rows == 0, carry_ref[...], pltpu.roll(z, 1, 0))
    carry_ref[...] = z[C - 1:C, :]
    zs = z + (z_prev - z) * mu_ref[...]
    r_all = zs[:, 0:MIX_WIDTH]
    k_all = zs[:, MIX_WIDTH:2 * MIX_WIDTH]
    v_all = zs[:, 2 * MIX_WIDTH:3 * MIX_WIDTH]
    wa = zs[:, 3 * MIX_WIDTH:3 * MIX_WIDTH + LORA_WA]
    gd = zs[:, 3 * MIX_WIDTH + LORA_WA:]

    w_log = -_softplus(-(w0_ref[...] + _dot(jnp.tanh(wa).astype(BF16), w2_ref[...]))) - 0.5
    logw = -jnp.exp(w_log)
    a_all = _sigmoid(a0_ref[...] + _dot(wa.astype(BF16), a2_ref[...]))
    g_all = _dot(_sigmoid(gd).astype(BF16), g2_ref[...])
    kkr_all = k_all * kk_ref[...]
    kmod_all = k_all * (1.0 + (a_all - 1.0) * ka_ref[...])

    if t_valid < C:
        valid = rows < t_valid
        logw = jnp.where(valid, logw, 0.0)
        r_all = jnp.where(valid, r_all, 0.0)
        kkr_all = jnp.where(valid, kkr_all, 0.0)
        kmod_all = jnp.where(valid, kmod_all, 0.0)
        v_all = jnp.where(valid, v_all, 0.0)

    tri_incl = (_iota2((C, C), 0) >= _iota2((C, C), 1)).astype(BF16)
    l_incl_all = _dot_const_l(tri_incl, logw)
    seg = _seg_ones(PAIR, HEAD_DIM)

    lane = _iota2((1, PAIR), 1)
    head0 = lane < HEAD_DIM
    r2 = _iota2((C2, C2), 0)
    c2 = _iota2((C2, C2), 1)
    same_head = (r2 // C) == (c2 // C)
    strict = jnp.logical_and(same_head, r2 > c2)
    incl = jnp.logical_and(same_head, r2 >= c2)
    sv = _iota2((PAIR, PAIR), 0) // HEAD_DIM
    sk = _iota2((PAIR, PAIR), 1) // HEAD_DIM
    state_mask = sv == sk

    def stack(x):
        return jnp.concatenate([jnp.where(head0, x, 0.0), jnp.where(head0, 0.0, x)], axis=0)

    def unstack(x):
        return x[:C] + x[C:]

    for p in range(N_PAIRS):
        sl = slice(p * PAIR, (p + 1) * PAIR)
        r_ = r_all[:, sl]
        v_ = v_all[:, sl]
        a_ = a_all[:, sl]
        km_ = kmod_all[:, sl]
        kkr_ = kkr_all[:, sl]
        kk_ = kkr_ * lax.rsqrt(jnp.maximum(_seg_sum(kkr_ * kkr_, seg), 1e-24))
        b_ = kk_ * a_
        l_incl = l_incl_all[:, sl]
        l_excl = l_incl - logw[:, sl]
        l_tot = l_incl[C - 1:C, :]
        e_neg = jnp.exp(-l_incl)
        e_rem = jnp.exp(l_tot - l_incl)
        a_t = -kk_ * jnp.exp(l_excl)
        b_t = b_ * e_neg
        k_t = km_ * e_neg
        r_t = r_ * jnp.exp(l_incl)

        la = stack(a_t).astype(BF16)
        lr = stack(r_t).astype(BF16)
        rb = jnp.concatenate([b_t, b_t], axis=0).astype(BF16)
        rk = jnp.concatenate([k_t, k_t], axis=0).astype(BF16)
        n_ab = jnp.where(strict, _dot_nt(la, rb), 0.0)
        a_ak = jnp.where(strict, _dot_nt(la, rk), 0.0).astype(BF16)
        a_rb = jnp.where(incl, _dot_nt(lr, rb), 0.0).astype(BF16)
        a_rk = jnp.where(incl, _dot_nt(lr, rk), 0.0).astype(BF16)
        t_inv = _unit_lower_inverse(n_ab, C2)

        s_prev = s_ref[p]
        s_bf = s_prev.astype(BF16)
        as0 = _dot_nt(a_t.astype(BF16), s_bf)
        rs0 = _dot_nt(r_t.astype(BF16), s_bf)
        v_st = stack(v_).astype(BF16)
        u_st = _mm_hi(t_inv, stack(as0) + _dot(a_ak, v_st))
        y_pair = rs0 + unstack(_dot(a_rb, u_st.astype(BF16)) + _dot(a_rk, v_st))
        u_pair = unstack(u_st)

        lhs = jnp.concatenate([u_pair, v_], axis=0).astype(BF16)
        rhs = jnp.concatenate([b_ * e_rem, km_ * e_rem], axis=0).astype(BF16)
        s_new = s_prev * jnp.exp(l_tot) + jnp.where(state_mask, _dot_tn(lhs, rhs), 0.0)
        s_ref[p] = s_new

        mean = _seg_sum(y_pair, seg) * (1.0 / HEAD_DIM)
        d = y_pair - mean
        var = _seg_sum(d * d, seg) * (1.0 / HEAD_DIM)
        yn = d * lax.rsqrt(var + GN_EPS) * lnw_ref[:, sl] + lnb_ref[:, sl]
        bonus = _seg_sum(r_ * km_ * rk_ref[:, sl], seg) * v_
        y_ref[0, :, sl] = ((yn + bonus) * g_all[:, sl]).astype(y_ref.dtype)

    @pl.when(c == n_chunks - 1)
    def _():
        sout_ref[0] = s_ref[...]


def _rwkv(z, sh0, s0, prm, t_valid):
    b, t_pad, _ = z.shape
    n_chunks = t_pad // CHUNK
    consts = [prm[k] for k in ("mu", "w0", "a0", "k_k", "k_a", "r_k", "ln_w", "ln_b", "w2p", "a2p", "g2")]
    kern = functools.partial(_rwkv_kernel, t_valid=t_valid)
    return pl.pallas_call(
        kern,
        grid=(b, n_chunks),
        in_specs=[pl.BlockSpec((1, CHUNK, RWKV_IN), lambda i, c: (i, c, 0)),
                  pl.BlockSpec((1, 1, RWKV_IN), lambda i, c: (i, 0, 0)),
                  pl.BlockSpec((1, N_PAIRS, PAIR, PAIR), lambda i, c: (i, 0, 0, 0))]
                 + [_const_spec(x.shape) for x in consts],
        out_specs=(pl.BlockSpec((1, CHUNK, MIX_WIDTH), lambda i, c: (i, c, 0)),
                   pl.BlockSpec((1, N_PAIRS, PAIR, PAIR), lambda i, c: (i, 0, 0, 0))),
        out_shape=(jax.ShapeDtypeStruct((b, t_pad, MIX_WIDTH), BF16),
                   jax.ShapeDtypeStruct((b, N_PAIRS, PAIR, PAIR), F32)),
        scratch_shapes=[pltpu.VMEM((1, RWKV_IN), F32), pltpu.VMEM((N_PAIRS, PAIR, PAIR), F32)],
        compiler_params=pltpu.CompilerParams(dimension_semantics=("arbitrary", "arbitrary"),
                                             vmem_limit_bytes=VMEM_LIMIT),
        name="rwkv",
    )(z, sh0, s0, *consts)


def _state_to_pairs(s):
    b = s.shape[0]
    s = s.reshape(b, N_PAIRS, 2, HEAD_DIM, HEAD_DIM)
    z = jnp.zeros_like(s[:, :, 0])
    top = jnp.concatenate([s[:, :, 0], z], axis=-1)
    bot = jnp.concatenate([z, s[:, :, 1]], axis=-1)
    return jnp.concatenate([top, bot], axis=-2)


def _pairs_to_state(sp):
    b = sp.shape[0]
    s0 = sp[:, :, :HEAD_DIM, :HEAD_DIM]
    s1 = sp[:, :, HEAD_DIM:, HEAD_DIM:]
    return jnp.stack([s0, s1], axis=2).reshape(b, 2 * N_PAIRS, HEAD_DIM, HEAD_DIM)


def _cumsum_kernel(lf_ref, c_ref):
    t = lf_ref.shape[1]
    tri = (_iota2((128, 128), 0) <= _iota2((128, 128), 1)).astype(BF16)
    carry = jnp.zeros((8, 1), F32)
    for j in range(t // 128):
        blk = _dot_const_r(lf_ref[:, j * 128:(j + 1) * 128], tri) + carry
        c_ref[:, j * 128:(j + 1) * 128] = blk
        carry = blk[:, 127:128]


def _cumsum_lanes(lft, b, t):
    return pl.pallas_call(
        _cumsum_kernel,
        grid=(b,),
        in_specs=[pl.BlockSpec((8, t), lambda i: (0, i))],
        out_specs=pl.BlockSpec((8, t), lambda i: (0, i)),
        out_shape=jax.ShapeDtypeStruct(lft.shape, F32),
        compiler_params=pltpu.CompilerParams(dimension_semantics=("arbitrary",)),
        name="cumsum",
    )(lft)


def _fox_prompt_kernel(q_ref, k_ref, v_ref, c_ref, fn_ref, o_ref, *, tq):
    p = pl.program_id(1)
    i = pl.program_id(2)
    lane = _iota2((1, PAIR), 1)
    head0 = lane < HEAD_DIM
    q = q_ref[0]
    zero = jnp.zeros_like(q)
    qs = (jnp.where(head0, q, zero), jnp.where(head0, zero, q))

    def block(j, carry, masked):
        start = pl.multiple_of(j * tq, tq)
        kb = k_ref[0, pl.ds(start, tq), :]
        vb = v_ref[0, pl.ds(start, tq), :]
        out = []
        for h in range(2):
            m, l, acc = carry[h]
            ck = c_ref[h:h + 1, pl.ds(start, tq)]
            s = _dot_nt(qs[h], kb) - ck
            if masked:
                s = jnp.where(_iota2((tq, tq), 0) >= _iota2((tq, tq), 1), s, NEG_BIG)
            m_new = jnp.maximum(m, jnp.max(s, axis=-1, keepdims=True))
            alpha = jnp.exp(m - m_new)
            pr = jnp.exp(s - m_new)
            l_new = alpha * l + jnp.sum(pr, axis=-1, keepdims=True)
            acc_new = alpha * acc + _dot(pr.astype(BF16), vb)
            out.append((m_new, l_new, acc_new))
        return tuple(out)

    init = tuple((jnp.full((tq, 1), NEG_BIG, F32), jnp.zeros((tq, 1), F32), jnp.zeros((tq, PAIR), F32))
                 for _ in range(2))
    carry = lax.fori_loop(0, i, lambda j, cr: block(j, cr, False), init)
    carry = block(i, carry, True)
    (_, l0, acc0), (_, l1, acc1) = carry
    o = jnp.where(head0, acc0 / l0, acc1 / l1)
    o2 = o * o
    ms0 = jnp.sum(jnp.where(head0, o2, 0.0), axis=-1, keepdims=True)
    ms1 = jnp.sum(jnp.where(head0, 0.0, o2), axis=-1, keepdims=True)
    ms = jnp.where(head0, ms0, ms1) * (1.0 / HEAD_DIM)
    o_ref[0] = (o * lax.rsqrt(ms + NORM_EPS) * fn_ref[...]).astype(o_ref.dtype)


def _fox_prompt(qb, kb, vb, c, fn, b, t, tq):
    kern = functools.partial(_fox_prompt_kernel, tq=tq)
    return pl.pallas_call(
        kern,
        grid=(b, N_PAIRS, t // tq),
        in_specs=[pl.BlockSpec((1, tq, PAIR), lambda bi, p, i: (bi, i, p)),
                  pl.BlockSpec((1, t, PAIR), lambda bi, p, i: (bi, 0, p)),
                  pl.BlockSpec((1, t, PAIR), lambda bi, p, i: (bi, 0, p)),
                  pl.BlockSpec((None, 2, t), lambda bi, p, i: (p, 0, bi)),
                  pl.BlockSpec((1, PAIR), lambda bi, p, i: (0, p))],
        out_specs=pl.BlockSpec((1, tq, PAIR), lambda bi, p, i: (bi, i, p)),
        out_shape=jax.ShapeDtypeStruct((b, t, MIX_WIDTH), BF16),
        compiler_params=pltpu.CompilerParams(
            dimension_semantics=("arbitrary", "arbitrary", "arbitrary"),
            vmem_limit_bytes=VMEM_LIMIT),
        name="fox_prompt",
    )(qb, kb, vb, c.reshape(N_PAIRS, 2, -1), fn)


def _fox_sample_kernel(pt_ref, q_ref, kn_ref, vn_ref, lfn_ref, kc_ref, vc_ref, lfc_ref, fn_ref,
                       o_ref, qbd_ref, m_ref, l_ref, acc_ref, carry_ref, *, t_new):
    j = pl.program_id(1)
    n_pages = pl.num_programs(1)
    n_heads = 2 * N_PAIRS
    n_rows = n_heads * t_new
    page = kc_ref.shape[0]
    after = (_iota2((page, page), 0) > _iota2((page, page), 1)).astype(BF16)
    lane_head = _iota2((1, MIX_WIDTH), 1) // HEAD_DIM

    def expand(d):
        return jnp.concatenate([jnp.broadcast_to(d[h:h + 1, :], (t_new, page)) for h in range(n_heads)], axis=0)

    def update(s, vb):
        m = m_ref[...]
        m_new = jnp.maximum(m, jnp.max(s, axis=-1, keepdims=True))
        alpha = jnp.exp(m - m_new)
        pr = jnp.exp(s - m_new)
        l_ref[...] = alpha * l_ref[...] + jnp.sum(pr, axis=-1, keepdims=True)
        acc_ref[...] = alpha * acc_ref[...] + _dot(pr.astype(BF16), vb)
        m_ref[...] = m_new

    @pl.when(j == 0)
    def _():
        q = q_ref[0].astype(F32)
        qbd = jnp.concatenate([jnp.where(lane_head == h, q, 0.0) for h in range(n_heads)], axis=0)
        qbd_ref[...] = qbd.astype(BF16)
        m_ref[...] = jnp.full(m_ref.shape, NEG_BIG, F32)
        l_ref[...] = jnp.zeros(l_ref.shape, F32)
        acc_ref[...] = jnp.zeros(acc_ref.shape, F32)
        pad = jnp.zeros((page - t_new, MIX_WIDTH), F32)
        kn = jnp.concatenate([kn_ref[0], pad], axis=0).astype(BF16)
        vn = jnp.concatenate([vn_ref[0], pad], axis=0).astype(BF16)
        lfn = lfn_ref[0]
        d = expand(_dot_const_r(lfn, after))
        s = _dot_nt(qbd_ref[...], kn) + d
        key = _iota2((n_rows, page), 1)
        qpos = _iota2((n_rows, page), 0) % t_new
        s = jnp.where(key <= qpos, s, NEG_BIG)
        update(s, vn)
        carry_ref[...] = jnp.sum(lfn, axis=-1, keepdims=True)

    lfc = lfc_ref[...]
    d = expand(_dot_const_r(lfc, after) + carry_ref[...])
    s = _dot_nt(qbd_ref[...], kc_ref[...].astype(BF16)) + d
    update(s, vc_ref[...].astype(BF16))
    carry_ref[...] = carry_ref[...] + jnp.sum(lfc, axis=-1, keepdims=True)

    @pl.when(j == n_pages - 1)
    def _():
        of = acc_ref[...] / l_ref[...]
        o = jnp.zeros((t_new, MIX_WIDTH), F32)
        for h in range(n_heads):
            o = o + jnp.where(lane_head == h, of[h * t_new:(h + 1) * t_new, :], 0.0)
        seg = _seg_ones(PAIR, HEAD_DIM)
        for p in range(N_PAIRS):
            sl = slice(p * PAIR, (p + 1) * PAIR)
            op = o[:, sl]
            ms = _seg_sum(op * op, seg) * (1.0 / HEAD_DIM)
            o_ref[0, :, sl] = (op * lax.rsqrt(ms + NORM_EPS) * fn_ref[:, sl]).astype(o_ref.dtype)


def _fox_sample(page_table, qb, kn, vn, lfn, cache_k, cache_v, cache_lft, fn, layer):
    b, t_new, _ = qb.shape
    n_pages = page_table.shape[1]
    page = cache_k.shape[2]
    n_rows = 2 * N_PAIRS * t_new
    kern = functools.partial(_fox_sample_kernel, t_new=t_new)
    rev = lambda bi, j, pt: pt[bi * n_pages + (n_pages - 1 - j)]
    grid_spec = pltpu.PrefetchScalarGridSpec(
        num_scalar_prefetch=1,
        grid=(b, n_pages),
        in_specs=[pl.BlockSpec((1, t_new, MIX_WIDTH), lambda bi, j, pt: (bi, 0, 0)),
                  pl.BlockSpec((1, t_new, MIX_WIDTH), lambda bi, j, pt: (bi, 0, 0)),
                  pl.BlockSpec((1, t_new, MIX_WIDTH), lambda bi, j, pt: (bi, 0, 0)),
                  pl.BlockSpec((1, 8, page), lambda bi, j, pt: (bi, 0, 0)),
                  pl.BlockSpec((None, None, page, MIX_WIDTH), lambda bi, j, pt: (layer, rev(bi, j, pt), 0, 0)),
                  pl.BlockSpec((None, None, page, MIX_WIDTH), lambda bi, j, pt: (layer, rev(bi, j, pt), 0, 0)),
                  pl.BlockSpec((None, None, 8, page), lambda bi, j, pt: (layer, rev(bi, j, pt), 0, 0)),
                  pl.BlockSpec((1, MIX_WIDTH), lambda bi, j, pt: (0, 0))],
        out_specs=pl.BlockSpec((1, t_new, MIX_WIDTH), lambda bi, j, pt: (bi, 0, 0)),
        scratch_shapes=[pltpu.VMEM((n_rows, MIX_WIDTH), BF16), pltpu.VMEM((n_rows, 1), F32),
                        pltpu.VMEM((n_rows, 1), F32), pltpu.VMEM((n_rows, MIX_WIDTH), F32),
                        pltpu.VMEM((8, 1), F32)],
    )
    return pl.pallas_call(
        kern,
        grid_spec=grid_spec,
        out_shape=jax.ShapeDtypeStruct((b, t_new, MIX_WIDTH), BF16),
        compiler_params=pltpu.CompilerParams(dimension_semantics=("arbitrary", "arbitrary"),
                                             vmem_limit_bytes=VMEM_LIMIT),
        name="fox_sample",
    )(page_table.reshape(-1), qb, kn, vn, lfn, cache_k, cache_v, cache_lft, fn)


def _layer_params(l, norm_mix, w_in, mu, w0, w2, a0, a2, g2, k_k, k_a, r_k, ln_w, ln_b, b_f,
                  fox_norm, w_out, norm_ffn, w_up, w_down):
    row = lambda x: x.reshape(1, -1)
    wi = w_in[l]
    n_h = b_f.shape[1]
    wf = wi[:, RWKV_IN + 3 * MIX_WIDTH:]
    zeros64 = jnp.zeros((64, MIX_WIDTH), F32)
    return dict(
        norm_mix=row(norm_mix[l]),
        wr=wi[:, :RWKV_IN].astype(BF16),
        wqkv=wi[:, RWKV_IN:RWKV_IN + 3 * MIX_WIDTH].astype(BF16),
        wf=jnp.pad(wf, ((0, 0), (0, 128 - n_h))).astype(BF16),
        wft=jnp.pad(wf.T, ((0, 16 - n_h), (0, 0))).astype(BF16),
        bfp=jnp.pad(row(b_f[l]), ((0, 0), (0, 128 - n_h))),
        bft=jnp.pad(b_f[l].reshape(-1, 1), ((0, 16 - n_h), (0, 0))),
        mu=row(mu[l]), w0=row(w0[l]), a0=row(a0[l]), k_k=row(k_k[l]), k_a=row(k_a[l]),
        r_k=row(r_k[l]), ln_w=row(ln_w[l]), ln_b=row(ln_b[l]),
        w2p=jnp.concatenate([w2[l], zeros64], axis=0).astype(BF16),
        a2p=jnp.concatenate([zeros64, a2[l]], axis=0).astype(BF16),
        g2=g2[l].astype(BF16),
        fox_norm=row(fox_norm[l]),
        w_out=w_out[l].astype(BF16),
        norm_ffn=row(norm_ffn[l]),
        w_up=w_up[l].astype(BF16),
        w_down=w_down[l].astype(BF16),
    )


def _pick_tile(n, pref):
    for t in pref:
        if n % t == 0:
            return t
    return n


def kernel(x_prompt, x_sample, cache_k, cache_v, cache_logf, state_wkv, state_shift, page_table,
           norm_mix, w_in, mu, w0, w2, a0, a2, g2, k_k, k_a, r_k, ln_w, ln_b, b_f, fox_norm,
           w_out, norm_ffn, w_up, w_down, norm_final):
    bp, tp, d = x_prompt.shape
    bs, ts, _ = x_sample.shape
    depth = w_in.shape[0]
    n_heads = b_f.shape[1]
    n_pool, page = cache_k.shape[1], cache_k.shape[2]
    np_tok, ns_tok = bp * tp, bs * ts
    tm_p = _pick_tile(np_tok, (512, 256, 128))
    tm_s = _pick_tile(ns_tok, (256, 128, 64, 32, 16, 8))
    tq = _pick_tile(tp, (128,))
    ts_pad = -(-ts // CHUNK) * CHUNK

    ck = cache_k.reshape(depth, n_pool, page, MIX_WIDTH)
    cv = cache_v.reshape(depth, n_pool, page, MIX_WIDTH)
    clft = jnp.swapaxes(cache_logf, 2, 3)
    nfin = norm_final.reshape(1, -1)

    xp = x_prompt.reshape(np_tok, d)
    xs = x_sample.reshape(ns_tok, d)
    zero_shift = jnp.zeros((bp, 1, RWKV_IN), F32)
    zero_state = jnp.zeros((bp, N_PAIRS, PAIR, PAIR), F32)
    outs_p, outs_s = [], []
    for l in range(depth):
        prm = _layer_params(l, norm_mix, w_in, mu, w0, w2, a0, a2, g2, k_k, k_a, r_k, ln_w, ln_b,
                            b_f, fox_norm, w_out, norm_ffn, w_up, w_down)
        final = l == depth - 1
        proj = lambda x, tm: _inproj(x, prm["norm_mix"], prm["wr"], prm["wqkv"], prm["wf"], prm["wft"],
                                     prm["bfp"], prm["bft"], tm)
        zr, k, v, qb, kb, vb, lf, lft = proj(xp, tm_p)
        zr3 = zr.reshape(bp, tp, RWKV_IN)
        y_r, s_fin = _rwkv(zr3, zero_shift, zero_state, prm, CHUNK)
        c = _cumsum_lanes(lft, bp, tp)
        o_f = _fox_prompt(qb.reshape(bp, tp, -1), kb.reshape(bp, tp, -1), vb.reshape(bp, tp, -1), c,
                          prm["fox_norm"], bp, tp, tq)
        xp = _outmlp(xp, y_r.reshape(np_tok, -1), o_f.reshape(np_tok, -1), prm["w_out"], prm["norm_ffn"],
                     prm["w_up"], prm["w_down"], nfin, tm_p, final)
        outs_p.append((k.reshape(bp, tp, n_heads, HEAD_DIM), v.reshape(bp, tp, n_heads, HEAD_DIM),
                       lf[:, :n_heads].reshape(bp, tp, n_heads), _pairs_to_state(s_fin), zr3[:, -1]))
        zr, k, v, qb, kb, vb, lf, lft = proj(xs, tm_s)
        zr3 = zr.reshape(bs, ts, RWKV_IN)
        zr_pad = jnp.pad(zr3, ((0, 0), (0, ts_pad - ts), (0, 0)))
        y_r, s_fin = _rwkv(zr_pad, state_shift[l][:, None, :], _state_to_pairs(state_wkv[l]), prm, ts)
        lfn = jnp.pad(jnp.swapaxes(lft.reshape(n_heads, bs, ts), 0, 1), ((0, 0), (0, 0), (0, page - ts)))
        o_f = _fox_sample(page_table, qb.reshape(bs, ts, -1), k.reshape(bs, ts, -1), v.reshape(bs, ts, -1),
                          lfn, ck, cv, clft, prm["fox_norm"], l)
        xs = _outmlp(xs, y_r[:, :ts].reshape(ns_tok, -1), o_f.reshape(ns_tok, -1), prm["w_out"],
                     prm["norm_ffn"], prm["w_up"], prm["w_down"], nfin, tm_s, final)
        outs_s.append((k.reshape(bs, ts, n_heads, HEAD_DIM), v.reshape(bs, ts, n_heads, HEAD_DIM),
                       lf[:, :n_heads].reshape(bs, ts, n_heads), _pairs_to_state(s_fin), zr3[:, -1]))
    stack = lambda outs, i: jnp.stack([o[i] for o in outs])
    return (xp.reshape(bp, tp, d), xs.reshape(bs, ts, d),
            stack(outs_p, 0), stack(outs_p, 1), stack(outs_p, 2), stack(outs_p, 3), stack(outs_p, 4),
            stack(outs_s, 0), stack(outs_s, 1), stack(outs_s, 2), stack(outs_s, 3), stack(outs_s, 4))
```

```python
import functools

import jax
import jax.numpy as jnp
from jax import lax
from jax.experimental import pallas as pl
from jax.experimental.pallas import tpu as pltpu

F32 = jnp.float32
BF16 = jnp.bfloat16

HEAD_DIM = 64
PAIR = 2 * HEAD_DIM
N_PAIRS = 4
MIX_WIDTH = 512
RWKV_IN = 1792
LORA_WA = 128
NORM_EPS = 1e-6
GN_EPS = 64e-5
CHUNK = 64
NEG_BIG = -1e30
VMEM_LIMIT = 56 * 1024 * 1024


def _dot(a, b):
    return jnp.dot(a, b, preferred_element_type=F32)


def _dot_nt(a, b):
    return lax.dot_general(a, b, (((1,), (1,)), ((), ())), preferred_element_type=F32)


def _dot_tn(a, b):
    return lax.dot_general(a, b, (((0,), (0,)), ((), ())), preferred_element_type=F32)


def _split2(x):
    hi = x.astype(BF16)
    lo = (x - hi.astype(F32)).astype(BF16)
    return hi, lo


def _split3(x):
    hi = x.astype(BF16)
    r1 = x - hi.astype(F32)
    mid = r1.astype(BF16)
    lo = (r1 - mid.astype(F32)).astype(BF16)
    return hi, mid, lo


def _dot_const_l(c, x):
    hi, mid, lo = _split3(x)
    return _dot(c, hi) + _dot(c, mid) + _dot(c, lo)


def _dot_const_r(x, c):
    hi, mid, lo = _split3(x)
    return _dot(hi, c) + _dot(mid, c) + _dot(lo, c)


def _mm_hi(a, b):
    ah, al = _split2(a)
    bh, bl = _split2(b)
    return _dot(ah, bh) + _dot(ah, bl) + _dot(al, bh)


def _rms(x, w):
    return x * lax.rsqrt(jnp.mean(x * x, axis=-1, keepdims=True) + NORM_EPS) * w


def _log_sigmoid(x):
    return jnp.minimum(x, 0.0) - jnp.log1p(jnp.exp(-jnp.abs(x)))


def _softplus(x):
    return jnp.maximum(x, 0.0) + jnp.log1p(jnp.exp(-jnp.abs(x)))


def _sigmoid(x):
    return 1.0 / (1.0 + jnp.exp(-x))


def _iota2(shape, dim):
    return lax.broadcasted_iota(jnp.int32, shape, dim)


def _seg_ones(n, seg):
    r = _iota2((n, n), 0) // seg
    c = _iota2((n, n), 1) // seg
    return (r == c).astype(BF16)


def _seg_sum(x, ones):
    hi, lo = _split2(x)
    return _dot(hi, ones) + _dot(lo, ones)


def _const_spec(shape):
    nd = len(shape)
    return pl.BlockSpec(shape, lambda *_: (0,) * nd)


def _inproj_kernel(x_ref, nw_ref, wr_ref, wqkv_ref, wf_ref, wft_ref, bf_ref, bft_ref,
                   zr_ref, k_ref, v_ref, qb_ref, kb_ref, vb_ref, lf_ref, lft_ref):
    x = x_ref[...]
    h = _rms(x, nw_ref[...]).astype(BF16)
    zr_ref[...] = _dot(h, wr_ref[...])
    qkv = _dot(h, wqkv_ref[...])
    q = qkv[:, :MIX_WIDTH]
    k = qkv[:, MIX_WIDTH:2 * MIX_WIDTH]
    v = qkv[:, 2 * MIX_WIDTH:]
    k_ref[...] = k
    v_ref[...] = v
    qb_ref[...] = (q * (HEAD_DIM ** -0.5)).astype(BF16)
    kb_ref[...] = k.astype(BF16)
    vb_ref[...] = v.astype(BF16)
    lf_ref[...] = _log_sigmoid(_dot(h, wf_ref[...]) + bf_ref[...])
    lft = _log_sigmoid(_dot_nt(wft_ref[...], h) + bft_ref[...])
    lft_ref[...] = lft[:8]


def _inproj(x, nw, wr, wqkv, wf, wft, bfp, bft, tm):
    n, d = x.shape
    grid = (n // tm,)
    row = lambda w: pl.BlockSpec((tm, w), lambda i: (i, 0))
    out_shape = (
        jax.ShapeDtypeStruct((n, RWKV_IN), F32),
        jax.ShapeDtypeStruct((n, MIX_WIDTH), F32),
        jax.ShapeDtypeStruct((n, MIX_WIDTH), F32),
        jax.ShapeDtypeStruct((n, MIX_WIDTH), BF16),
        jax.ShapeDtypeStruct((n, MIX_WIDTH), BF16),
        jax.ShapeDtypeStruct((n, MIX_WIDTH), BF16),
        jax.ShapeDtypeStruct((n, 128), F32),
        jax.ShapeDtypeStruct((8, n), F32),
    )
    return pl.pallas_call(
        _inproj_kernel,
        grid=grid,
        in_specs=[row(d), _const_spec(nw.shape), _const_spec(wr.shape), _const_spec(wqkv.shape),
                  _const_spec(wf.shape), _const_spec(wft.shape), _const_spec(bfp.shape),
                  _const_spec(bft.shape)],
        out_specs=(row(RWKV_IN), row(MIX_WIDTH), row(MIX_WIDTH), row(MIX_WIDTH), row(MIX_WIDTH),
                   row(MIX_WIDTH), row(128), pl.BlockSpec((8, tm), lambda i: (0, i))),
        out_shape=out_shape,
        compiler_params=pltpu.CompilerParams(dimension_semantics=("arbitrary",),
                                             vmem_limit_bytes=VMEM_LIMIT),
        name="inproj",
    )(x, nw, wr, wqkv, wf, wft, bfp, bft)


def _outmlp_kernel(x_ref, yr_ref, of_ref, wo_ref, nf_ref, wu_ref, wd_ref, nfin_ref, o_ref,
                   *, ff_tile, final):
    wo = wo_ref
    x = x_ref[...] + _dot(yr_ref[...], wo[:MIX_WIDTH, :]) + _dot(of_ref[...], wo[MIX_WIDTH:, :])
    h2 = _rms(x, nf_ref[...]).astype(BF16)
    mlp = None
    d_ff = wu_ref.shape[1]
    for j in range(d_ff // ff_tile):
        u = _dot(h2, wu_ref[:, j * ff_tile:(j + 1) * ff_tile])
        u = jnp.square(jnp.maximum(u, 0.0)).astype(BF16)
        t = _dot(u, wd_ref[j * ff_tile:(j + 1) * ff_tile, :])
        mlp = t if mlp is None else mlp + t
    acc = x + mlp
    if final:
        acc = _rms(acc, nfin_ref[...])
    o_ref[...] = acc


def _outmlp(x, yr, of, wo, nf, wu, wd, nfin, tm, final):
    n, d = x.shape
    grid = (n // tm,)
    row = lambda w: pl.BlockSpec((tm, w), lambda i: (i, 0))
    kern = functools.partial(_outmlp_kernel, ff_tile=1024, final=final)
    return pl.pallas_call(
        kern,
        grid=grid,
        in_specs=[row(d), row(MIX_WIDTH), row(MIX_WIDTH), _const_spec(wo.shape), _const_spec(nf.shape),
                  _const_spec(wu.shape), _const_spec(wd.shape), _const_spec(nfin.shape)],
        out_specs=row(d),
        out_shape=jax.ShapeDtypeStruct((n, d), F32),
        compiler_params=pltpu.CompilerParams(dimension_semantics=("arbitrary",),
                                             vmem_limit_bytes=VMEM_LIMIT),
        name="outmlp",
    )(x, yr, of, wo, nf, wu, wd, nfin)


def _unit_lower_inverse(n_mats, c):
    r = _iota2((c, c), 0)
    col = _iota2((c, c), 1)
    eye = (r == col).astype(F32)
    same8 = (r // 8) == (col // 8)
    n8 = [jnp.where(same8, n, 0.0) for n in n_mats]
    n8_2 = [_mm_hi(n, n) for n in n8]
    ts = [eye + n for n in n8]
    n8_4 = [_mm_hi(n, n) for n in n8_2]
    ts = [t + _mm_hi(t, n) for t, n in zip(ts, n8_2)]
    ts = [t + _mm_hi(t, n) for t, n in zip(ts, n8_4)]
    size = 8
    while size < c // 2:
        same_lo = (r // size) == (col // size)
        same_hi = (r // (2 * size)) == (col // (2 * size))
        off = jnp.logical_and(same_hi, jnp.logical_not(same_lo))
        xs = [_mm_hi(jnp.where(off, n, 0.0), t) for n, t in zip(n_mats, ts)]
        ts = [t + _mm_hi(t, x) for t, x in zip(ts, xs)]
        size *= 2
    return ts


def _rwkv_kernel(z_ref, sh0_ref, s0_ref, mu_ref, w0_ref, a0_ref, kk_ref, ka_ref, rk_ref,
                 lnw_ref, lnb_ref, w2_ref, a2_ref, g2_ref,
                 y_ref, sout_ref, carry_ref, s_ref, *, t_valid):
    c = pl.program_id(1)
    n_chunks = pl.num_programs(1)
    C = z_ref.shape[1]
    C2 = 2 * C

    @pl.when(c == 0)
    def _():
        carry_ref[...] = sh0_ref[0]
        s_ref[...] = s0_ref[0]

    z = z_ref[0]
    rows = _iota2((C, 1), 0)
    z_prev = jnp.where(rows == 0, carry_ref[...], pltpu.roll(z, 1, 0))
    carry_ref[...] = z[C - 1:C, :]
    zs = z + (z_prev - z) * mu_ref[...]
    r_all = zs[:, 0:MIX_WIDTH]
    k_all = zs[:, MIX_WIDTH:2 * MIX_WIDTH]
    v_all = zs[:, 2 * MIX_WIDTH:3 * MIX_WIDTH]
    wa = zs[:, 3 * MIX_WIDTH:3 * MIX_WIDTH + LORA_WA]
    gd = zs[:, 3 * MIX_WIDTH + LORA_WA:]

    w_log = -_softplus(-(w0_ref[...] + _dot(jnp.tanh(wa).astype(BF16), w2_ref[...]))) - 0.5
    logw = -jnp.exp(w_log)
    a_all = _sigmoid(a0_ref[...] + _dot(wa.astype(BF16), a2_ref[...]))
    g_all = _dot(_sigmoid(gd).astype(BF16), g2_ref[...])
    kkr_all = k_all * kk_ref[...]
    kmod_all = k_all * (1.0 + (a_all - 1.0) * ka_ref[...])

    if t_valid < C:
        valid = rows < t_valid
        logw = jnp.where(valid, logw, 0.0)
        r_all = jnp.where(valid, r_all, 0.0)
        kkr_all = jnp.where(valid, kkr_all, 0.0)
        kmod_all = jnp.where(valid, kmod_all, 0.0)
        v_all = jnp.where(valid, v_all, 0.0)

    tri_incl = (_iota2((C, C), 0) >= _iota2((C, C), 1)).astype(BF16)
    l_incl_all = _dot_const_l(tri_incl, logw)
    seg = _seg_ones(PAIR, HEAD_DIM)

    lane = _iota2((1, PAIR), 1)
    head0 = lane < HEAD_DIM
    r2 = _iota2((C2, C2), 0)
    c2 = _iota2((C2, C2), 1)
    same_head = (r2 // C) == (c2 // C)
    strict = jnp.logical_and(same_head, r2 > c2)
    incl = jnp.logical_and(same_head, r2 >= c2)
    sv = _iota2((PAIR, PAIR), 0) // HEAD_DIM
    sk = _iota2((PAIR, PAIR), 1) // HEAD_DIM
    state_mask = sv == sk

    def stack(x):
        return jnp.concatenate([jnp.where(head0, x, 0.0), jnp.where(head0, 0.0, x)], axis=0)

    def unstack(x):
        return x[:C] + x[C:]

    pairs = range(N_PAIRS)
    sls = [slice(p * PAIR, (p + 1) * PAIR) for p in pairs]
    r_ = [r_all[:, sl] for sl in sls]
    v_ = [v_all[:, sl] for sl in sls]
    km_ = [kmod_all[:, sl] for sl in sls]
    kkr_ = [kkr_all[:, sl] for sl in sls]
    ss_ = [_seg_sum(x * x, seg) for x in kkr_]
    kk_ = [x * lax.rsqrt(jnp.maximum(s, 1e-24)) for x, s in zip(kkr_, ss_)]
    b_ = [kk * a_all[:, sl] for kk, sl in zip(kk_, sls)]
    l_incl = [l_incl_all[:, sl] for sl in sls]
    l_tot = [l[C - 1:C, :] for l in l_incl]
    e_neg = [jnp.exp(-l) for l in l_incl]
    e_rem = [jnp.exp(lt - l) for lt, l in zip(l_tot, l_incl)]
    a_t = [-kk * jnp.exp(l - logw[:, sl]) for kk, l, sl in zip(kk_, l_incl, sls)]
    r_t = [r * jnp.exp(l) for r, l in zip(r_, l_incl)]
    la = [stack(x).astype(BF16) for x in a_t]
    lr = [stack(x).astype(BF16) for x in r_t]
    rb = [jnp.concatenate([x, x], axis=0).astype(BF16) for x in (b * e for b, e in zip(b_, e_neg))]
    rk = [jnp.concatenate([x, x], axis=0).astype(BF16) for x in (k * e for k, e in zip(km_, e_neg))]
    n_ab = [jnp.where(strict, _dot_nt(x, y), 0.0) for x, y in zip(la, rb)]
    a_ak = [jnp.where(strict, _dot_nt(x, y), 0.0).astype(BF16) for x, y in zip(la, rk)]
    a_rb = [jnp.where(incl, _dot_nt(x, y), 0.0).astype(BF16) for x, y in zip(lr, rb)]
    a_rk = [jnp.where(incl, _dot_nt(x, y), 0.0).astype(BF16) for x, y in zip(lr, rk)]

    s_prev = [s_ref[p] for p in pairs]
    s_bf = [s.astype(BF16) for s in s_prev]
    as0 = [_dot_nt(x.astype(BF16), s) for x, s in zip(a_t, s_bf)]
    rs0 = [_dot_nt(x.astype(BF16), s) for x, s in zip(r_t, s_bf)]
    v_st = [stack(x).astype(BF16) for x in v_]
    rhs_u = [stack(x) + _dot(a, v) for x, a, v in zip(as0, a_ak, v_st)]
    y_v = [_dot(a, v) for a, v in zip(a_rk, v_st)]
    t_inv = _unit_lower_inverse(n_ab, C2)
    u_st = [_mm_hi(t, x) for t, x in zip(t_inv, rhs_u)]
    y_pair = [rs + unstack(_dot(a, u.astype(BF16)) + yv) for rs, a, u, yv in zip(rs0, a_rb, u_st, y_v)]
    lhs = [jnp.concatenate([unstack(u), v], axis=0).astype(BF16) for u, v in zip(u_st, v_)]
    rhs = [jnp.concatenate([b * e, k * e], axis=0).astype(BF16) for b, k, e in zip(b_, km_, e_rem)]
    upd = [jnp.where(state_mask, _dot_tn(x, y), 0.0) for x, y in zip(lhs, rhs)]
    for p in pairs:
        s_ref[p] = s_prev[p] * jnp.exp(l_tot[p]) + upd[p]

    mean = [_seg_sum(y, seg) * (1.0 / HEAD_DIM) for y in y_pair]
    dev = [y - m for y, m in zip(y_pair, mean)]
    var = [_seg_sum(d * d, seg) * (1.0 / HEAD_DIM) for d in dev]
    bonus = [_seg_sum(r * k * rk_ref[:, sl], seg) * v for r, k, v, sl in zip(r_, km_, v_, sls)]
    for p in pairs:
        sl = sls[p]
        yn = dev[p] * lax.rsqrt(var[p] + GN_EPS) * lnw_ref[:, sl] + lnb_ref[:, sl]
        y_ref[0, :, sl] = ((yn + bonus[p]) * g_all[:, sl]).astype(y_ref.dtype)

    @pl.when(c == n_chunks - 1)
    def _():
        sout_ref[0] = s_ref[...]


def _rwkv(z, sh0, s0, prm, t_valid):
    b, t_pad, _ = z.shape
    n_chunks = t_pad // CHUNK
    consts = [prm[k] for k in ("mu", "w0", "a0", "k_k", "k_a", "r_k", "ln_w", "ln_b", "w2p", "a2p", "g2")]
    kern = functools.partial(_rwkv_kernel, t_valid=t_valid)
    return pl.pallas_call(
        kern,
        grid=(b, n_chunks),
        in_specs=[pl.BlockSpec((1, CHUNK, RWKV_IN), lambda i, c: (i, c, 0)),
                  pl.BlockSpec((1, 1, RWKV_IN), lambda i, c: (i, 0, 0)),
                  pl.BlockSpec((1, N_PAIRS, PAIR, PAIR), lambda i, c: (i, 0, 0, 0))]
                 + [_const_spec(x.shape) for x in consts],
        out_specs=(pl.BlockSpec((1, CHUNK, MIX_WIDTH), lambda i, c: (i, c, 0)),
                   pl.BlockSpec((1, N_PAIRS, PAIR, PAIR), lambda i, c: (i, 0, 0, 0))),
        out_shape=(jax.ShapeDtypeStruct((b, t_pad, MIX_WIDTH), BF16),
                   jax.ShapeDtypeStruct((b, N_PAIRS, PAIR, PAIR), F32)),
        scratch_shapes=[pltpu.VMEM((1, RWKV_IN), F32), pltpu.VMEM((N_PAIRS, PAIR, PAIR), F32)],
        compiler_params=pltpu.CompilerParams(dimension_semantics=("arbitrary", "arbitrary"),
                                             vmem_limit_bytes=VMEM_LIMIT),
        name="rwkv",
    )(z, sh0, s0, *consts)


def _state_to_pairs(s):
    b = s.shape[0]
    s = s.reshape(b, N_PAIRS, 2, HEAD_DIM, HEAD_DIM)
    z = jnp.zeros_like(s[:, :, 0])
    top = jnp.concatenate([s[:, :, 0], z], axis=-1)
    bot = jnp.concatenate([z, s[:, :, 1]], axis=-1)
    return jnp.concatenate([top, bot], axis=-2)


def _pairs_to_state(sp):
    b = sp.shape[0]
    s0 = sp[:, :, :HEAD_DIM, :HEAD_DIM]
    s1 = sp[:, :, HEAD_DIM:, HEAD_DIM:]
    return jnp.stack([s0, s1], axis=2).reshape(b, 2 * N_PAIRS, HEAD_DIM, HEAD_DIM)


def _cumsum_kernel(lf_ref, c_ref):
    t = lf_ref.shape[1]
    tri = (_iota2((128, 128), 0) <= _iota2((128, 128), 1)).astype(BF16)
    carry = jnp.zeros((8, 1), F32)
    for j in range(t // 128):
        blk = _dot_const_r(lf_ref[:, j * 128:(j + 1) * 128], tri) + carry
        c_ref[:, j * 128:(j + 1) * 128] = blk
        carry = blk[:, 127:128]


def _cumsum_lanes(lft, b, t):
    return pl.pallas_call(
        _cumsum_kernel,
        grid=(b,),
        in_specs=[pl.BlockSpec((8, t), lambda i: (0, i))],
        out_specs=pl.BlockSpec((8, t), lambda i: (0, i)),
        out_shape=jax.ShapeDtypeStruct(lft.shape, F32),
        compiler_params=pltpu.CompilerParams(dimension_semantics=("arbitrary",)),
        name="cumsum",
    )(lft)


def _fox_prompt_kernel(q_ref, k_ref, v_ref, c_ref, fn_ref, o_ref, m_ref, l_ref, acc_ref, *, tq):
    p = pl.program_id(1)
    i = pl.program_id(2)
    lane = _iota2((1, PAIR), 1)
    head0 = lane < HEAD_DIM
    q = q_ref[0]
    zero = jnp.zeros_like(q)
    qs = (jnp.where(head0, q, zero), jnp.where(head0, zero, q))

    m_ref[...] = jnp.full(m_ref.shape, NEG_BIG, F32)
    l_ref[...] = jnp.zeros(l_ref.shape, F32)
    acc_ref[...] = jnp.zeros(acc_ref.shape, F32)

    def block(j, masked):
        start = pl.multiple_of(j * tq, tq)
        kb = k_ref[0, pl.ds(start, tq), :]
        vb = v_ref[0, pl.ds(start, tq), :]
        for h in range(2):
            ck = c_ref[h:h + 1, pl.ds(start, tq)]
            s = _dot_nt(qs[h], kb) - ck
            if masked:
                s = jnp.where(_iota2((tq, tq), 0) >= _iota2((tq, tq), 1), s, NEG_BIG)
            m = m_ref[h]
            m_new = jnp.maximum(m, jnp.max(s, axis=-1, keepdims=True))
            alpha = jnp.exp(m - m_new)
            pr = jnp.exp(s - m_new)
            l_ref[h] = alpha * l_ref[h] + jnp.sum(pr, axis=-1, keepdims=True)
            acc_ref[h] = alpha * acc_ref[h] + _dot(pr.astype(BF16), vb)
            m_ref[h] = m_new

    def body(j, carry):
        block(j, False)
        return carry

    lax.fori_loop(0, i, body, 0)
    block(i, True)
    o = jnp.where(head0, acc_ref[0] / l_ref[0], acc_ref[1] / l_ref[1])
    o2 = o * o
    ms0 = jnp.sum(jnp.where(head0, o2, 0.0), axis=-1, keepdims=True)
    ms1 = jnp.sum(jnp.where(head0, 0.0, o2), axis=-1, keepdims=True)
    ms = jnp.where(head0, ms0, ms1) * (1.0 / HEAD_DIM)
    o_ref[0] = (o * lax.rsqrt(ms + NORM_EPS) * fn_ref[...]).astype(o_ref.dtype)


def _fox_prompt(qb, kb, vb, c, fn, b, t, tq):
    kern = functools.partial(_fox_prompt_kernel, tq=tq)
    return pl.pallas_call(
        kern,
        grid=(b, N_PAIRS, t // tq),
        in_specs=[pl.BlockSpec((1, tq, PAIR), lambda bi, p, i: (bi, i, p)),
                  pl.BlockSpec((1, t, PAIR), lambda bi, p, i: (bi, 0, p)),
                  pl.BlockSpec((1, t, PAIR), lambda bi, p, i: (bi, 0, p)),
                  pl.BlockSpec((None, 2, t), lambda bi, p, i: (p, 0, bi)),
                  pl.BlockSpec((1, PAIR), lambda bi, p, i: (0, p))],
        out_specs=pl.BlockSpec((1, tq, PAIR), lambda bi, p, i: (bi, i, p)),
        out_shape=jax.ShapeDtypeStruct((b, t, MIX_WIDTH), BF16),
        scratch_shapes=[pltpu.VMEM((2, tq, 1), F32), pltpu.VMEM((2, tq, 1), F32),
                        pltpu.VMEM((2, tq, PAIR), F32)],
        compiler_params=pltpu.CompilerParams(
            dimension_semantics=("arbitrary", "arbitrary", "arbitrary"),
            vmem_limit_bytes=VMEM_LIMIT),
        name="fox_prompt",
    )(qb, kb, vb, c.reshape(N_PAIRS, 2, -1), fn)


PAGES_PER_STEP = 8


def _fox_sample_kernel(pt_ref, q_ref, kn_ref, vn_ref, lfn_ref, *rest, t_new, group):
    kc_refs = rest[:group]
    vc_refs = rest[group:2 * group]
    lfc_refs = rest[2 * group:3 * group]
    fn_ref, o_ref, qbd_ref, m_ref, l_ref, acc_ref, carry_ref = rest[3 * group:]
    j = pl.program_id(1)
    n_steps = pl.num_programs(1)
    n_heads = 2 * N_PAIRS
    n_rows = n_heads * t_new
    page = lfn_ref.shape[2]
    after = (_iota2((page, page), 0) > _iota2((page, page), 1)).astype(BF16)
    lane_head = _iota2((1, MIX_WIDTH), 1) // HEAD_DIM

    def expand(d):
        n = d.shape[1]
        return jnp.concatenate([jnp.broadcast_to(d[h:h + 1, :], (t_new, n)) for h in range(n_heads)], axis=0)

    def repack(ref):
        heads = [ref[pl.ds(h, page, stride=n_heads), :] for h in range(n_heads)]
        return jnp.concatenate(heads, axis=1).astype(BF16)

    def update(s, vb):
        m = m_ref[...]
        m_new = jnp.maximum(m, jnp.max(s, axis=-1, keepdims=True))
        alpha = jnp.exp(m - m_new)
        pr = jnp.exp(s - m_new)
        l_ref[...] = alpha * l_ref[...] + jnp.sum(pr, axis=-1, keepdims=True)
        acc_ref[...] = alpha * acc_ref[...] + _dot(pr.astype(BF16), vb)
        m_ref[...] = m_new

    @pl.when(j == 0)
    def _():
        q = q_ref[0].astype(F32)
        qbd = jnp.concatenate([jnp.where(lane_head == h, q, 0.0) for h in range(n_heads)], axis=0)
        qbd_ref[...] = qbd.astype(BF16)
        m_ref[...] = jnp.full(m_ref.shape, NEG_BIG, F32)
        l_ref[...] = jnp.zeros(l_ref.shape, F32)
        acc_ref[...] = jnp.zeros(acc_ref.shape, F32)
        pad = jnp.zeros((page - t_new, MIX_WIDTH), F32)
        kn = jnp.concatenate([kn_ref[0], pad], axis=0).astype(BF16)
        vn = jnp.concatenate([vn_ref[0], pad], axis=0).astype(BF16)
        lfn = lfn_ref[0]
        d = expand(_dot_const_r(lfn, after))
        s = _dot_nt(qbd_ref[...], kn) + d
        key = _iota2((n_rows, page), 1)
        qpos = _iota2((n_rows, page), 0) % t_new
        s = jnp.where(key <= qpos, s, NEG_BIG)
        update(s, vn)
        carry_ref[...] = jnp.sum(lfn, axis=-1, keepdims=True)

    carry = carry_ref[...]
    ds = []
    for g in range(group):
        lfc = lfc_refs[g][...]
        ds.append(_dot_const_r(lfc, after) + carry)
        carry = carry + jnp.sum(lfc, axis=-1, keepdims=True)
    carry_ref[...] = carry
    k_all = jnp.concatenate([repack(r) for r in kc_refs], axis=0)
    v_all = jnp.concatenate([repack(r) for r in vc_refs], axis=0)
    s = _dot_nt(qbd_ref[...], k_all) + expand(jnp.concatenate(ds, axis=1))
    update(s, v_all)

    @pl.when(j == n_steps - 1)
    def _():
        of = acc_ref[...] / l_ref[...]
        o = jnp.zeros((t_new, MIX_WIDTH), F32)
        for h in range(n_heads):
            o = o + jnp.where(lane_head == h, of[h * t_new:(h + 1) * t_new, :], 0.0)
        seg = _seg_ones(PAIR, HEAD_DIM)
        for p in range(N_PAIRS):
            sl = slice(p * PAIR, (p + 1) * PAIR)
            op = o[:, sl]
            ms = _seg_sum(op * op, seg) * (1.0 / HEAD_DIM)
            o_ref[0, :, sl] = (op * lax.rsqrt(ms + NORM_EPS) * fn_ref[:, sl]).astype(o_ref.dtype)


def _fox_sample(page_table, qb, kn, vn, lfn, cache_k, cache_v, cache_lft, fn, layer):
    b, t_new, _ = qb.shape
    n_pages = page_table.shape[1]
    page = cache_lft.shape[3]
    rows = cache_k.shape[2]
    n_rows = 2 * N_PAIRS * t_new
    group = PAGES_PER_STEP if n_pages % PAGES_PER_STEP == 0 else 1
    kern = functools.partial(_fox_sample_kernel, t_new=t_new, group=group)

    def page_idx(g):
        return lambda bi, j, pt: (layer, pt[bi * n_pages + (n_pages - 1 - (j * group + g))], 0, 0)

    tok = lambda bi, j, pt: (bi, 0, 0)
    grid_spec = pltpu.PrefetchScalarGridSpec(
        num_scalar_prefetch=1,
        grid=(b, n_pages // group),
        in_specs=[pl.BlockSpec((1, t_new, MIX_WIDTH), tok), pl.BlockSpec((1, t_new, MIX_WIDTH), tok),
                  pl.BlockSpec((1, t_new, MIX_WIDTH), tok), pl.BlockSpec((1, 8, page), tok)]
                 + [pl.BlockSpec((None, None, rows, HEAD_DIM), page_idx(g)) for g in range(group)]
                 + [pl.BlockSpec((None, None, rows, HEAD_DIM), page_idx(g)) for g in range(group)]
                 + [pl.BlockSpec((None, None, 8, page), page_idx(g)) for g in range(group)]
                 + [pl.BlockSpec((1, MIX_WIDTH), lambda bi, j, pt: (0, 0))],
        out_specs=pl.BlockSpec((1, t_new, MIX_WIDTH), tok),
        scratch_shapes=[pltpu.VMEM((n_rows, MIX_WIDTH), BF16), pltpu.VMEM((n_rows, 1), F32),
                        pltpu.VMEM((n_rows, 1), F32), pltpu.VMEM((n_rows, MIX_WIDTH), F32),
                        pltpu.VMEM((8, 1), F32)],
    )
    return pl.pallas_call(
        kern,
        grid_spec=grid_spec,
        out_shape=jax.ShapeDtypeStruct((b, t_new, MIX_WIDTH), BF16),
        compiler_params=pltpu.CompilerParams(dimension_semantics=("arbitrary", "arbitrary"),
                                             vmem_limit_bytes=VMEM_LIMIT),
        name="fox_sample",
    )(page_table.reshape(-1), qb, kn, vn, lfn, *([cache_k] * group), *([cache_v] * group),
      *([cache_lft] * group), fn)


def _layer_params(l, norm_mix, w_in, mu, w0, w2, a0, a2, g2, k_k, k_a, r_k, ln_w, ln_b, b_f,
                  fox_norm, w_out, norm_ffn, w_up, w_down):
    row = lambda x: x.reshape(1, -1)
    wi = w_in[l]
    n_h = b_f.shape[1]
    wf = wi[:, RWKV_IN + 3 * MIX_WIDTH:]
    zeros64 = jnp.zeros((64, MIX_WIDTH), F32)
    return dict(
        norm_mix=row(norm_mix[l]),
        wr=wi[:, :RWKV_IN].astype(BF16),
        wqkv=wi[:, RWKV_IN:RWKV_IN + 3 * MIX_WIDTH].astype(BF16),
        wf=jnp.pad(wf, ((0, 0), (0, 128 - n_h))).astype(BF16),
        wft=jnp.pad(wf.T, ((0, 16 - n_h), (0, 0))).astype(BF16),
        bfp=jnp.pad(row(b_f[l]), ((0, 0), (0, 128 - n_h))),
        bft=jnp.pad(b_f[l].reshape(-1, 1), ((0, 16 - n_h), (0, 0))),
        mu=row(mu[l]), w0=row(w0[l]), a0=row(a0[l]), k_k=row(k_k[l]), k_a=row(k_a[l]),
        r_k=row(r_k[l]), ln_w=row(ln_w[l]), ln_b=row(ln_b[l]),
        w2p=jnp.concatenate([w2[l], zeros64], axis=0).astype(BF16),
        a2p=jnp.concatenate([zeros64, a2[l]], axis=0).astype(BF16),
        g2=g2[l].astype(BF16),
        fox_norm=row(fox_norm[l]),
        w_out=w_out[l].astype(BF16),
        norm_ffn=row(norm_ffn[l]),
        w_up=w_up[l].astype(BF16),
        w_down=w_down[l].astype(BF16),
    )


def _pick_tile(n, pref):
    for t in pref:
        if n % t == 0:
            return t
    return n


def kernel(x_prompt, x_sample, cache_k, cache_v, cache_logf, state_wkv, state_shift, page_table,
           norm_mix, w_in, mu, w0, w2, a0, a2, g2, k_k, k_a, r_k, ln_w, ln_b, b_f, fox_norm,
           w_out, norm_ffn, w_up, w_down, norm_final):
    bp, tp, d = x_prompt.shape
    bs, ts, _ = x_sample.shape
    depth = w_in.shape[0]
    n_heads = b_f.shape[1]
    n_pool, page = cache_k.shape[1], cache_k.shape[2]
    np_tok, ns_tok = bp * tp, bs * ts
    tm_p = _pick_tile(np_tok, (512, 256, 128))
    tm_s = _pick_tile(ns_tok, (256, 128, 64, 32, 16, 8))
    tq = _pick_tile(tp, (512, 256, 128))
    ts_pad = -(-ts // CHUNK) * CHUNK

    ck = cache_k.reshape(depth, n_pool, page * n_heads, HEAD_DIM)
    cv = cache_v.reshape(depth, n_pool, page * n_heads, HEAD_DIM)
    clft = jnp.swapaxes(cache_logf, 2, 3)
    nfin = norm_final.reshape(1, -1)

    xp = x_prompt.reshape(np_tok, d)
    xs = x_sample.reshape(ns_tok, d)
    zero_shift = jnp.zeros((bp, 1, RWKV_IN), F32)
    zero_state = jnp.zeros((bp, N_PAIRS, PAIR, PAIR), F32)
    outs_p, outs_s = [], []
    for l in range(depth):
        prm = _layer_params(l, norm_mix, w_in, mu, w0, w2, a0, a2, g2, k_k, k_a, r_k, ln_w, ln_b,
                            b_f, fox_norm, w_out, norm_ffn, w_up, w_down)
        final = l == depth - 1
        proj = lambda x, tm: _inproj(x, prm["norm_mix"], prm["wr"], prm["wqkv"], prm["wf"], prm["wft"],
                                     prm["bfp"], prm["bft"], tm)
        zr, k, v, qb, kb, vb, lf, lft = proj(xp, tm_p)
        zr3 = zr.reshape(bp, tp, RWKV_IN)
        y_r, s_fin = _rwkv(zr3, zero_shift, zero_state, prm, CHUNK)
        c = _cumsum_lanes(lft, bp, tp)
        o_f = _fox_prompt(qb.reshape(bp, tp, -1), kb.reshape(bp, tp, -1), vb.reshape(bp, tp, -1), c,
                          prm["fox_norm"], bp, tp, tq)
        xp = _outmlp(xp, y_r.reshape(np_tok, -1), o_f.reshape(np_tok, -1), prm["w_out"], prm["norm_ffn"],
                     prm["w_up"], prm["w_down"], nfin, tm_p, final)
        outs_p.append((k.reshape(bp, tp, n_heads, HEAD_DIM), v.reshape(bp, tp, n_heads, HEAD_DIM),
                       lf[:, :n_heads].reshape(bp, tp, n_heads), _pairs_to_state(s_fin), zr3[:, -1]))
        zr, k, v, qb, kb, vb, lf, lft = proj(xs, tm_s)
        zr3 = zr.reshape(bs, ts, RWKV_IN)
        zr_pad = jnp.pad(zr3, ((0, 0), (0, ts_pad - ts), (0, 0)))
        y_r, s_fin = _rwkv(zr_pad, state_shift[l][:, None, :], _state_to_pairs(state_wkv[l]), prm, ts)
        lfn = jnp.pad(jnp.swapaxes(lft.reshape(n_heads, bs, ts), 0, 1), ((0, 0), (0, 0), (0, page - ts)))
        o_f = _fox_sample(page_table, qb.reshape(bs, ts, -1), k.reshape(bs, ts, -1), v.reshape(bs, ts, -1),
                          lfn, ck, cv, clft, prm["fox_norm"], l)
        xs = _outmlp(xs, y_r[:, :ts].reshape(ns_tok, -1), o_f.reshape(ns_tok, -1), prm["w_out"],
                     prm["norm_ffn"], prm["w_up"], prm["w_down"], nfin, tm_s, final)
        outs_s.append((k.reshape(bs, ts, n_heads, HEAD_DIM), v.reshape(bs, ts, n_heads, HEAD_DIM),
                       lf[:, :n_heads].reshape(bs, ts, n_heads), _pairs_to_state(s_fin), zr3[:, -1]))
    stack = lambda outs, i: jnp.stack([o[i] for o in outs])
    return (xp.reshape(bp, tp, d), xs.reshape(bs, ts, d),
            stack(outs_p, 0), stack(outs_p, 1), stack(outs_p, 2), stack(outs_p, 3), stack(outs_p, 4),
            stack(outs_s, 0), stack(outs_s, 1), stack(outs_s, 2), stack(outs_s, 3), stack(outs_s, 4))
```

```python
import functools

import jax
import jax.numpy as jnp
from jax import lax
from jax.experimental import pallas as pl
from jax.experimental.pallas import tpu as pltpu

F32 = jnp.float32
BF16 = jnp.bfloat16

HEAD_DIM = 64
PAIR = 2 * HEAD_DIM
N_PAIRS = 4
MIX_WIDTH = 512
RWKV_IN = 1792
LORA_WA = 128
NORM_EPS = 1e-6
GN_EPS = 64e-5
CHUNK = 64
NEG_BIG = -1e30
LOG2E = 1.4426950408889634
VMEM_LIMIT = 56 * 1024 * 1024


def _dot(a, b):
    return jnp.dot(a, b, preferred_element_type=F32)


def _dot_nt(a, b):
    return lax.dot_general(a, b, (((1,), (1,)), ((), ())), preferred_element_type=F32)


def _dot_tn(a, b):
    return lax.dot_general(a, b, (((0,), (0,)), ((), ())), preferred_element_type=F32)


def _split2(x):
    hi = x.astype(BF16)
    lo = (x - hi.astype(F32)).astype(BF16)
    return hi, lo


def _split3(x):
    hi = x.astype(BF16)
    r1 = x - hi.astype(F32)
    mid = r1.astype(BF16)
    lo = (r1 - mid.astype(F32)).astype(BF16)
    return hi, mid, lo


def _dot_const_l(c, x):
    hi, mid, lo = _split3(x)
    return _dot(c, hi) + _dot(c, mid) + _dot(c, lo)


def _dot_const_r(x, c):
    hi, mid, lo = _split3(x)
    return _dot(hi, c) + _dot(mid, c) + _dot(lo, c)


def _mm_hi(a, b):
    ah, al = _split2(a)
    bh, bl = _split2(b)
    return _dot(ah, bh) + _dot(ah, bl) + _dot(al, bh)


def _rms(x, w):
    return x * lax.rsqrt(jnp.mean(x * x, axis=-1, keepdims=True) + NORM_EPS) * w


def _log_sigmoid(x):
    return jnp.minimum(x, 0.0) - jnp.log1p(jnp.exp(-jnp.abs(x)))


def _softplus(x):
    return jnp.maximum(x, 0.0) + jnp.log1p(jnp.exp(-jnp.abs(x)))


def _sigmoid(x):
    return 1.0 / (1.0 + jnp.exp(-x))


def _iota2(shape, dim):
    return lax.broadcasted_iota(jnp.int32, shape, dim)


def _seg_ones(n, seg):
    r = _iota2((n, n), 0) // seg
    c = _iota2((n, n), 1) // seg
    return (r == c).astype(BF16)


def _seg_sum(x, ones):
    hi, lo = _split2(x)
    return _dot(hi, ones) + _dot(lo, ones)


def _const_spec(shape):
    nd = len(shape)
    return pl.BlockSpec(shape, lambda *_: (0,) * nd)


def _inproj_kernel(x_ref, nw_ref, wr_ref, wqkv_ref, wkvt_ref, wft_ref, bft_ref,
                   zr_ref, k_ref, v_ref, qb_ref, kb_ref, vb_ref, lft_ref, *, token_minor):
    x = x_ref[...]
    h = _rms(x, nw_ref[...]).astype(BF16)
    zr_ref[...] = _dot(h, wr_ref[...])
    qkv = _dot(h, wqkv_ref[...])
    q = qkv[:, :MIX_WIDTH]
    k = qkv[:, MIX_WIDTH:2 * MIX_WIDTH]
    v = qkv[:, 2 * MIX_WIDTH:]
    if token_minor:
        kvt = _dot_nt(wkvt_ref[...], h)
        k_ref[...] = kvt[:MIX_WIDTH]
        v_ref[...] = kvt[MIX_WIDTH:]
    else:
        k_ref[...] = k
        v_ref[...] = v
    qb_ref[...] = (q * (HEAD_DIM ** -0.5 * LOG2E)).astype(BF16)
    kb_ref[...] = k.astype(BF16)
    vb_ref[...] = v.astype(BF16)
    lft = _log_sigmoid(_dot_nt(wft_ref[...], h) + bft_ref[...])
    lft_ref[...] = lft[:8]


def _inproj(x, nw, wr, wqkv, wkvt, wft, bft, tm, seq_len, token_minor):
    n, d = x.shape
    grid = (n // tm,)
    row = lambda w: pl.BlockSpec((tm, w), lambda i: (i, 0))
    if token_minor:
        nb = seq_len // tm
        kv_shape = jax.ShapeDtypeStruct((n // seq_len, MIX_WIDTH, seq_len), F32)
        kv_spec = pl.BlockSpec((None, MIX_WIDTH, tm), lambda i: (i // nb, 0, i % nb))
    else:
        kv_shape = jax.ShapeDtypeStruct((n, MIX_WIDTH), F32)
        kv_spec = row(MIX_WIDTH)
    out_shape = (
        jax.ShapeDtypeStruct((n, RWKV_IN), F32), kv_shape, kv_shape,
        jax.ShapeDtypeStruct((n, MIX_WIDTH), BF16),
        jax.ShapeDtypeStruct((n, MIX_WIDTH), BF16),
        jax.ShapeDtypeStruct((n, MIX_WIDTH), BF16),
        jax.ShapeDtypeStruct((8, n), F32),
    )
    return pl.pallas_call(
        functools.partial(_inproj_kernel, token_minor=token_minor),
        grid=grid,
        in_specs=[row(d), _const_spec(nw.shape), _const_spec(wr.shape), _const_spec(wqkv.shape),
                  _const_spec(wkvt.shape), _const_spec(wft.shape), _const_spec(bft.shape)],
        out_specs=(row(RWKV_IN), kv_spec, kv_spec, row(MIX_WIDTH), row(MIX_WIDTH),
                   row(MIX_WIDTH), pl.BlockSpec((8, tm), lambda i: (0, i))),
        out_shape=out_shape,
        compiler_params=pltpu.CompilerParams(dimension_semantics=("arbitrary",),
                                             vmem_limit_bytes=VMEM_LIMIT),
        name="inproj",
    )(x, nw, wr, wqkv, wkvt, wft, bft)


def _outmlp_kernel(x_ref, yr_ref, of_ref, wo_ref, nf_ref, wu_ref, wd_ref, nfin_ref, o_ref,
                   *, ff_tile, final):
    wo = wo_ref
    x = x_ref[...] + _dot(yr_ref[...], wo[:MIX_WIDTH, :]) + _dot(of_ref[...], wo[MIX_WIDTH:, :])
    h2 = _rms(x, nf_ref[...]).astype(BF16)
    mlp = None
    d_ff = wu_ref.shape[1]
    for j in range(d_ff // ff_tile):
        u = _dot(h2, wu_ref[:, j * ff_tile:(j + 1) * ff_tile])
        u = jnp.square(jnp.maximum(u, 0.0)).astype(BF16)
        t = _dot(u, wd_ref[j * ff_tile:(j + 1) * ff_tile, :])
        mlp = t if mlp is None else mlp + t
    acc = x + mlp
    if final:
        acc = _rms(acc, nfin_ref[...])
    o_ref[...] = acc


def _outmlp(x, yr, of, wo, nf, wu, wd, nfin, tm, final):
    n, d = x.shape
    grid = (n // tm,)
    row = lambda w: pl.BlockSpec((tm, w), lambda i: (i, 0))
    kern = functools.partial(_outmlp_kernel, ff_tile=1024, final=final)
    return pl.pallas_call(
        kern,
        grid=grid,
        in_specs=[row(d), row(MIX_WIDTH), row(MIX_WIDTH), _const_spec(wo.shape), _const_spec(nf.shape),
                  _const_spec(wu.shape), _const_spec(wd.shape), _const_spec(nfin.shape)],
        out_specs=row(d),
        out_shape=jax.ShapeDtypeStruct((n, d), F32),
        compiler_params=pltpu.CompilerParams(dimension_semantics=("arbitrary",),
                                             vmem_limit_bytes=VMEM_LIMIT),
        name="outmlp",
    )(x, yr, of, wo, nf, wu, wd, nfin)


def _unit_lower_inverse(n_mats, c):
    r = _iota2((c, c), 0)
    col = _iota2((c, c), 1)
    eye = (r == col).astype(F32)
    same8 = (r // 8) == (col // 8)
    n8 = [jnp.where(same8, n, 0.0) for n in n_mats]
    n8_2 = [_mm_hi(n, n) for n in n8]
    ts = [eye + n for n in n8]
    n8_4 = [_mm_hi(n, n) for n in n8_2]
    ts = [t + _mm_hi(t, n) for t, n in zip(ts, n8_2)]
    ts = [t + _mm_hi(t, n) for t, n in zip(ts, n8_4)]
    size = 8
    while size < c // 2:
        same_lo = (r // size) == (col // size)
        same_hi = (r // (2 * size)) == (col // (2 * size))
        off = jnp.logical_and(same_hi, jnp.logical_not(same_lo))
        xs = [_mm_hi(jnp.where(off, n, 0.0), t) for n, t in zip(n_mats, ts)]
        ts = [t + _mm_hi(t, x) for t, x in zip(ts, xs)]
        size *= 2
    return ts


def _rwkv_kernel(z_ref, sh0_ref, s0_ref, mu_ref, w0_ref, a0_ref, kk_ref, ka_ref, rk_ref,
                 lnw_ref, lnb_ref, w2_ref, a2_ref, g2_ref,
                 y_ref, sout_ref, carry_ref, s_ref, *, t_valid):
    c = pl.program_id(1)
    n_chunks = pl.num_programs(1)
    C = z_ref.shape[1]
    C2 = 2 * C

    @pl.when(c == 0)
    def _():
        carry_ref[...] = sh0_ref[0]
        s_ref[...] = s0_ref[0]

    z = z_ref[0]
    rows = _iota2((C, 1), 0)
    z_prev = jnp.where(rows == 0, carry_ref[...], pltpu.roll(z, 1, 0))
    carry_ref[...] = z[C - 1:C, :]
    zs = z + (z_prev - z) * mu_ref[...]
    r_all = zs[:, 0:MIX_WIDTH]
    k_all = zs[:, MIX_WIDTH:2 * MIX_WIDTH]
    v_all = zs[:, 2 * MIX_WIDTH:3 * MIX_WIDTH]
    wa = zs[:, 3 * MIX_WIDTH:3 * MIX_WIDTH + LORA_WA]
    gd = zs[:, 3 * MIX_WIDTH + LORA_WA:]

    w_log = -_softplus(-(w0_ref[...] + _dot(jnp.tanh(wa).astype(BF16), w2_ref[...]))) - 0.5
    logw = -jnp.exp(w_log)
    a_all = _sigmoid(a0_ref[...] + _dot(wa.astype(BF16), a2_ref[...]))
    g_all = _dot(_sigmoid(gd).astype(BF16), g2_ref[...])
    kkr_all = k_all * kk_ref[...]
    kmod_all = k_all * (1.0 + (a_all - 1.0) * ka_ref[...])

    if t_valid < C:
        valid = rows < t_valid
        logw = jnp.where(valid, logw, 0.0)
        r_all = jnp.where(valid, r_all, 0.0)
        kkr_all = jnp.where(valid, kkr_all, 0.0)
        kmod_all = jnp.where(valid, kmod_all, 0.0)
        v_all = jnp.where(valid, v_all, 0.0)

    tri_incl = (_iota2((C, C), 0) >= _iota2((C, C), 1)).astype(BF16)
    l_incl_all = _dot_const_l(tri_incl, logw)
    seg = _seg_ones(PAIR, HEAD_DIM)

    lane = _iota2((1, PAIR), 1)
    head0 = lane < HEAD_DIM
    r2 = _iota2((C2, C2), 0)
    c2 = _iota2((C2, C2), 1)
    same_head = (r2 // C) == (c2 // C)
    strict = jnp.logical_and(same_head, r2 > c2)
    incl = jnp.logical_and(same_head, r2 >= c2)
    sv = _iota2((PAIR, PAIR), 0) // HEAD_DIM
    sk = _iota2((PAIR, PAIR), 1) // HEAD_DIM
    state_mask = sv == sk

    def stack(x):
        return jnp.concatenate([jnp.where(head0, x, 0.0), jnp.where(head0, 0.0, x)], axis=0)

    def unstack(x):
        return x[:C] + x[C:]

    pairs = range(N_PAIRS)
    sls = [slice(p * PAIR, (p + 1) * PAIR) for p in pairs]
    r_ = [r_all[:, sl] for sl in sls]
    v_ = [v_all[:, sl] for sl in sls]
    km_ = [kmod_all[:, sl] for sl in sls]
    kkr_ = [kkr_all[:, sl] for sl in sls]
    ss_ = [_seg_sum(x * x, seg) for x in kkr_]
    kk_ = [x * lax.rsqrt(jnp.maximum(s, 1e-24)) for x, s in zip(kkr_, ss_)]
    b_ = [kk * a_all[:, sl] for kk, sl in zip(kk_, sls)]
    l_incl = [l_incl_all[:, sl] for sl in sls]
    l_tot = [l[C - 1:C, :] for l in l_incl]
    e_neg = [jnp.exp(-l) for l in l_incl]
    e_rem = [jnp.exp(lt - l) for lt, l in zip(l_tot, l_incl)]
    a_t = [-kk * jnp.exp(l - logw[:, sl]) for kk, l, sl in zip(kk_, l_incl, sls)]
    r_t = [r * jnp.exp(l) for r, l in zip(r_, l_incl)]
    la = [stack(x).astype(BF16) for x in a_t]
    lr = [stack(x).astype(BF16) for x in r_t]
    rb = [jnp.concatenate([x, x], axis=0).astype(BF16) for x in (b * e for b, e in zip(b_, e_neg))]
    rk = [jnp.concatenate([x, x], axis=0).astype(BF16) for x in (k * e for k, e in zip(km_, e_neg))]
    n_ab = [jnp.where(strict, _dot_nt(x, y), 0.0) for x, y in zip(la, rb)]
    a_ak = [jnp.where(strict, _dot_nt(x, y), 0.0).astype(BF16) for x, y in zip(la, rk)]
    a_rb = [jnp.where(incl, _dot_nt(x, y), 0.0).astype(BF16) for x, y in zip(lr, rb)]
    a_rk = [jnp.where(incl, _dot_nt(x, y), 0.0).astype(BF16) for x, y in zip(lr, rk)]

    s_prev = [s_ref[p] for p in pairs]
    s_bf = [s.astype(BF16) for s in s_prev]
    as0 = [_dot_nt(x.astype(BF16), s) for x, s in zip(a_t, s_bf)]
    rs0 = [_dot_nt(x.astype(BF16), s) for x, s in zip(r_t, s_bf)]
    v_st = [stack(x).astype(BF16) for x in v_]
    rhs_u = [stack(x) + _dot(a, v) for x, a, v in zip(as0, a_ak, v_st)]
    y_v = [_dot(a, v) for a, v in zip(a_rk, v_st)]
    t_inv = _unit_lower_inverse(n_ab, C2)
    u_st = [_mm_hi(t, x) for t, x in zip(t_inv, rhs_u)]
    y_pair = [rs + unstack(_dot(a, u.astype(BF16)) + yv) for rs, a, u, yv in zip(rs0, a_rb, u_st, y_v)]
    lhs = [jnp.concatenate([unstack(u), v], axis=0).astype(BF16) for u, v in zip(u_st, v_)]
    rhs = [jnp.concatenate([b * e, k * e], axis=0).astype(BF16) for b, k, e in zip(b_, km_, e_rem)]
    upd = [jnp.where(state_mask, _dot_tn(x, y), 0.0) for x, y in zip(lhs, rhs)]
    for p in pairs:
        s_ref[p] = s_prev[p] * jnp.exp(l_tot[p]) + upd[p]

    mean = [_seg_sum(y, seg) * (1.0 / HEAD_DIM) for y in y_pair]
    dev = [y - m for y, m in zip(y_pair, mean)]
    var = [_seg_sum(d * d, seg) * (1.0 / HEAD_DIM) for d in dev]
    bonus = [_seg_sum(r * k * rk_ref[:, sl], seg) * v for r, k, v, sl in zip(r_, km_, v_, sls)]
    for p in pairs:
        sl = sls[p]
        yn = dev[p] * lax.rsqrt(var[p] + GN_EPS) * lnw_ref[:, sl] + lnb_ref[:, sl]
        y_ref[0, :, sl] = ((yn + bonus[p]) * g_all[:, sl]).astype(y_ref.dtype)

    @pl.when(c == n_chunks - 1)
    def _():
        sout_ref[0] = s_ref[...]


def _rwkv(z, sh0, s0, prm, t_valid):
    b, t_pad, _ = z.shape
    n_chunks = t_pad // CHUNK
    consts = [prm[k] for k in ("mu", "w0", "a0", "k_k", "k_a", "r_k", "ln_w", "ln_b", "w2p", "a2p", "g2")]
    kern = functools.partial(_rwkv_kernel, t_valid=t_valid)
    return pl.pallas_call(
        kern,
        grid=(b, n_chunks),
        in_specs=[pl.BlockSpec((1, CHUNK, RWKV_IN), lambda i, c: (i, c, 0)),
                  pl.BlockSpec((1, 1, RWKV_IN), lambda i, c: (i, 0, 0)),
                  pl.BlockSpec((1, N_PAIRS, PAIR, PAIR), lambda i, c: (i, 0, 0, 0))]
                 + [_const_spec(x.shape) for x in consts],
        out_specs=(pl.BlockSpec((1, CHUNK, MIX_WIDTH), lambda i, c: (i, c, 0)),
                   pl.BlockSpec((1, N_PAIRS, PAIR, PAIR), lambda i, c: (i, 0, 0, 0))),
        out_shape=(jax.ShapeDtypeStruct((b, t_pad, MIX_WIDTH), BF16),
                   jax.ShapeDtypeStruct((b, N_PAIRS, PAIR, PAIR), F32)),
        scratch_shapes=[pltpu.VMEM((1, RWKV_IN), F32), pltpu.VMEM((N_PAIRS, PAIR, PAIR), F32)],
        compiler_params=pltpu.CompilerParams(dimension_semantics=("arbitrary", "arbitrary"),
                                             vmem_limit_bytes=VMEM_LIMIT),
        name="rwkv",
    )(z, sh0, s0, *consts)


def _state_to_pairs(s):
    b = s.shape[0]
    s = s.reshape(b, N_PAIRS, 2, HEAD_DIM, HEAD_DIM)
    z = jnp.zeros_like(s[:, :, 0])
    top = jnp.concatenate([s[:, :, 0], z], axis=-1)
    bot = jnp.concatenate([z, s[:, :, 1]], axis=-1)
    return jnp.concatenate([top, bot], axis=-2)


def _pairs_to_state(sp):
    b = sp.shape[0]
    s0 = sp[:, :, :HEAD_DIM, :HEAD_DIM]
    s1 = sp[:, :, HEAD_DIM:, HEAD_DIM:]
    return jnp.stack([s0, s1], axis=2).reshape(b, 2 * N_PAIRS, HEAD_DIM, HEAD_DIM)


def _cumsum_kernel(lf_ref, c_ref):
    t = lf_ref.shape[1]
    tri = (_iota2((128, 128), 0) <= _iota2((128, 128), 1)).astype(BF16)
    carry = jnp.zeros((8, 1), F32)
    for j in range(t // 128):
        blk = _dot_const_r(lf_ref[:, j * 128:(j + 1) * 128], tri) + carry
        c_ref[:, j * 128:(j + 1) * 128] = blk
        carry = blk[:, 127:128]


def _cumsum_lanes(lft, b, t):
    return pl.pallas_call(
        _cumsum_kernel,
        grid=(b,),
        in_specs=[pl.BlockSpec((8, t), lambda i: (0, i))],
        out_specs=pl.BlockSpec((8, t), lambda i: (0, i)),
        out_shape=jax.ShapeDtypeStruct(lft.shape, F32),
        compiler_params=pltpu.CompilerParams(dimension_semantics=("arbitrary",)),
        name="cumsum",
    )(lft)


def _fox_prompt_kernel(q_ref, k_ref, v_ref, c_ref, fn_ref, o_ref, m_ref, acc_ref, *, tq):
    p = pl.program_id(1)
    i = pl.program_id(2)
    lane = _iota2((1, PAIR), 1)
    head0 = lane < HEAD_DIM
    q = q_ref[0]
    zero = jnp.zeros_like(q)
    qs = (jnp.where(head0, q, zero), jnp.where(head0, zero, q))

    m_ref[...] = jnp.full(m_ref.shape, NEG_BIG, F32)
    acc_ref[...] = jnp.zeros(acc_ref.shape, F32)

    def block(j, masked):
        start = pl.multiple_of(j * tq, tq)
        kb = k_ref[0, pl.ds(start, tq), :]
        vb = v_ref[0, pl.ds(start, tq), :]
        one = jnp.ones_like(vb)
        vs = (jnp.where(head0, vb, one), jnp.where(head0, one, vb))
        for h in range(2):
            ck = c_ref[h:h + 1, pl.ds(start, tq)] * LOG2E
            s = _dot_nt(qs[h], kb) - ck
            if masked:
                s = jnp.where(_iota2((tq, tq), 0) >= _iota2((tq, tq), 1), s, NEG_BIG)
            m = m_ref[h]
            m_new = jnp.maximum(m, jnp.max(s, axis=-1, keepdims=True))
            alpha = jnp.exp2(m - m_new)
            pr = jnp.exp2(s - m_new)
            acc_ref[h] = alpha * acc_ref[h] + _dot(pr.astype(BF16), vs[h])
            m_ref[h] = m_new

    def body(j, carry):
        block(j, False)
        return carry

    lax.fori_loop(0, i, body, 0)
    block(i, True)
    acc0 = acc_ref[0]
    acc1 = acc_ref[1]
    o = jnp.where(head0, acc0 / pltpu.roll(acc0, HEAD_DIM, 1), acc1 / pltpu.roll(acc1, HEAD_DIM, 1))
    o2 = o * o
    ms0 = jnp.sum(jnp.where(head0, o2, 0.0), axis=-1, keepdims=True)
    ms1 = jnp.sum(jnp.where(head0, 0.0, o2), axis=-1, keepdims=True)
    ms = jnp.where(head0, ms0, ms1) * (1.0 / HEAD_DIM)
    o_ref[0] = (o * lax.rsqrt(ms + NORM_EPS) * fn_ref[...]).astype(o_ref.dtype)


def _fox_prompt(qb, kb, vb, c, fn, b, t, tq):
    kern = functools.partial(_fox_prompt_kernel, tq=tq)
    return pl.pallas_call(
        kern,
        grid=(b, N_PAIRS, t // tq),
        in_specs=[pl.BlockSpec((1, tq, PAIR), lambda bi, p, i: (bi, i, p)),
                  pl.BlockSpec((1, t, PAIR), lambda bi, p, i: (bi, 0, p)),
                  pl.BlockSpec((1, t, PAIR), lambda bi, p, i: (bi, 0, p)),
                  pl.BlockSpec((None, 2, t), lambda bi, p, i: (p, 0, bi)),
                  pl.BlockSpec((1, PAIR), lambda bi, p, i: (0, p))],
        out_specs=pl.BlockSpec((1, tq, PAIR), lambda bi, p, i: (bi, i, p)),
        out_shape=jax.ShapeDtypeStruct((b, t, MIX_WIDTH), BF16),
        scratch_shapes=[pltpu.VMEM((2, tq, 1), F32), pltpu.VMEM((2, tq, PAIR), F32)],
        compiler_params=pltpu.CompilerParams(
            dimension_semantics=("arbitrary", "arbitrary", "arbitrary"),
            vmem_limit_bytes=VMEM_LIMIT),
        name="fox_prompt",
    )(qb, kb, vb, c.reshape(N_PAIRS, 2, -1), fn)


PAGES_PER_STEP = 8


def _fox_sample_kernel(pt_ref, q_ref, kn_ref, vn_ref, lfn_ref, *rest, t_new, group):
    kc_refs = rest[:group]
    vc_refs = rest[group:2 * group]
    lfc_refs = rest[2 * group:3 * group]
    fn_ref, o_ref, qbd_ref, m_ref, l_ref, acc_ref, carry_ref = rest[3 * group:]
    j = pl.program_id(1)
    n_steps = pl.num_programs(1)
    n_heads = 2 * N_PAIRS
    n_rows = n_heads * t_new
    page = lfn_ref.shape[2]
    after = (_iota2((page, page), 0) > _iota2((page, page), 1)).astype(BF16)
    lane_head = _iota2((1, MIX_WIDTH), 1) // HEAD_DIM

    def expand(d):
        n = d.shape[1]
        return jnp.concatenate([jnp.broadcast_to(d[h:h + 1, :], (t_new, n)) for h in range(n_heads)], axis=0)

    def update(s, vb, v_token_minor):
        m = m_ref[...]
        m_new = jnp.maximum(m, jnp.max(s, axis=-1, keepdims=True))
        alpha = jnp.exp2(m - m_new)
        pr = jnp.exp2(s - m_new)
        l_ref[...] = alpha * l_ref[...] + jnp.sum(pr, axis=-1, keepdims=True)
        pv = _dot_nt(pr.astype(BF16), vb) if v_token_minor else _dot(pr.astype(BF16), vb)
        acc_ref[...] = alpha * acc_ref[...] + pv
        m_ref[...] = m_new

    @pl.when(j == 0)
    def _():
        q = q_ref[0].astype(F32)
        qbd = jnp.concatenate([jnp.where(lane_head == h, q, 0.0) for h in range(n_heads)], axis=0)
        qbd_ref[...] = qbd.astype(BF16)
        m_ref[...] = jnp.full(m_ref.shape, NEG_BIG, F32)
        l_ref[...] = jnp.zeros(l_ref.shape, F32)
        acc_ref[...] = jnp.zeros(acc_ref.shape, F32)
        pad = jnp.zeros((page - t_new, MIX_WIDTH), F32)
        kn = jnp.concatenate([kn_ref[0], pad], axis=0).astype(BF16)
        vn = jnp.concatenate([vn_ref[0], pad], axis=0).astype(BF16)
        lfn = lfn_ref[0]
        d = expand(_dot_const_r(lfn, after) * LOG2E)
        s = _dot_nt(qbd_ref[...], kn) + d
        key = _iota2((n_rows, page), 1)
        qpos = _iota2((n_rows, page), 0) % t_new
        s = jnp.where(key <= qpos, s, NEG_BIG)
        update(s, vn, False)
        carry_ref[...] = jnp.sum(lfn, axis=-1, keepdims=True)

    carry = carry_ref[...]
    ds = []
    for g in range(group):
        lfc = lfc_refs[g][...]
        ds.append((_dot_const_r(lfc, after) + carry) * LOG2E)
        carry = carry + jnp.sum(lfc, axis=-1, keepdims=True)
    carry_ref[...] = carry
    kt_all = jnp.concatenate([r[...].astype(BF16) for r in kc_refs], axis=1)
    vt_all = jnp.concatenate([r[...].astype(BF16) for r in vc_refs], axis=1)
    s = _dot(qbd_ref[...], kt_all) + expand(jnp.concatenate(ds, axis=1))
    update(s, vt_all, True)

    @pl.when(j == n_steps - 1)
    def _():
        of = acc_ref[...] / l_ref[...]
        o = jnp.zeros((t_new, MIX_WIDTH), F32)
        for h in range(n_heads):
            o = o + jnp.where(lane_head == h, of[h * t_new:(h + 1) * t_new, :], 0.0)
        seg = _seg_ones(PAIR, HEAD_DIM)
        for p in range(N_PAIRS):
            sl = slice(p * PAIR, (p + 1) * PAIR)
            op = o[:, sl]
            ms = _seg_sum(op * op, seg) * (1.0 / HEAD_DIM)
            o_ref[0, :, sl] = (op * lax.rsqrt(ms + NORM_EPS) * fn_ref[:, sl]).astype(o_ref.dtype)


def _fox_sample(page_table, qb, kn, vn, lfn, cache_k, cache_v, cache_lft, fn, layer):
    b, t_new, _ = qb.shape
    n_pages = page_table.shape[1]
    page = cache_lft.shape[3]
    n_rows = 2 * N_PAIRS * t_new
    group = PAGES_PER_STEP if n_pages % PAGES_PER_STEP == 0 else 1
    kern = functools.partial(_fox_sample_kernel, t_new=t_new, group=group)

    def page_idx(g):
        return lambda bi, j, pt: (layer, pt[bi * n_pages + (n_pages - 1 - (j * group + g))], 0, 0)

    tok = lambda bi, j, pt: (bi, 0, 0)
    grid_spec = pltpu.PrefetchScalarGridSpec(
        num_scalar_prefetch=1,
        grid=(b, n_pages // group),
        in_specs=[pl.BlockSpec((1, t_new, MIX_WIDTH), tok), pl.BlockSpec((1, t_new, MIX_WIDTH), tok),
                  pl.BlockSpec((1, t_new, MIX_WIDTH), tok), pl.BlockSpec((1, 8, page), tok)]
                 + [pl.BlockSpec((None, None, MIX_WIDTH, page), page_idx(g)) for g in range(group)]
                 + [pl.BlockSpec((None, None, MIX_WIDTH, page), page_idx(g)) for g in range(group)]
                 + [pl.BlockSpec((None, None, 8, page), page_idx(g)) for g in range(group)]
                 + [pl.BlockSpec((1, MIX_WIDTH), lambda bi, j, pt: (0, 0))],
        out_specs=pl.BlockSpec((1, t_new, MIX_WIDTH), tok),
        scratch_shapes=[pltpu.VMEM((n_rows, MIX_WIDTH), BF16), pltpu.VMEM((n_rows, 1), F32),
                        pltpu.VMEM((n_rows, 1), F32), pltpu.VMEM((n_rows, MIX_WIDTH), F32),
                        pltpu.VMEM((8, 1), F32)],
    )
    return pl.pallas_call(
        kern,
        grid_spec=grid_spec,
        out_shape=jax.ShapeDtypeStruct((b, t_new, MIX_WIDTH), BF16),
        compiler_params=pltpu.CompilerParams(dimension_semantics=("arbitrary", "arbitrary"),
                                             vmem_limit_bytes=VMEM_LIMIT),
        name="fox_sample",
    )(page_table.reshape(-1), qb, kn, vn, lfn, *([cache_k] * group), *([cache_v] * group),
      *([cache_lft] * group), fn)


def _layer_params(l, norm_mix, w_in, mu, w0, w2, a0, a2, g2, k_k, k_a, r_k, ln_w, ln_b, b_f,
                  fox_norm, w_out, norm_ffn, w_up, w_down):
    row = lambda x: x.reshape(1, -1)
    wi = w_in[l]
    n_h = b_f.shape[1]
    wf = wi[:, RWKV_IN + 3 * MIX_WIDTH:]
    zeros64 = jnp.zeros((64, MIX_WIDTH), F32)
    return dict(
        norm_mix=row(norm_mix[l]),
        wr=wi[:, :RWKV_IN].astype(BF16),
        wqkv=wi[:, RWKV_IN:RWKV_IN + 3 * MIX_WIDTH].astype(BF16),
        wkvt=wi[:, RWKV_IN + MIX_WIDTH:RWKV_IN + 3 * MIX_WIDTH].T.astype(BF16),
        wft=jnp.pad(wf.T, ((0, 16 - n_h), (0, 0))).astype(BF16),
        bft=jnp.pad(b_f[l].reshape(-1, 1), ((0, 16 - n_h), (0, 0))),
        mu=row(mu[l]), w0=row(w0[l]), a0=row(a0[l]), k_k=row(k_k[l]), k_a=row(k_a[l]),
        r_k=row(r_k[l]), ln_w=row(ln_w[l]), ln_b=row(ln_b[l]),
        w2p=jnp.concatenate([w2[l], zeros64], axis=0).astype(BF16),
        a2p=jnp.concatenate([zeros64, a2[l]], axis=0).astype(BF16),
        g2=g2[l].astype(BF16),
        fox_norm=row(fox_norm[l]),
        w_out=w_out[l].astype(BF16),
        norm_ffn=row(norm_ffn[l]),
        w_up=w_up[l].astype(BF16),
        w_down=w_down[l].astype(BF16),
    )


def _pick_tile(n, pref):
    for t in pref:
        if n % t == 0:
            return t
    return n


def kernel(x_prompt, x_sample, cache_k, cache_v, cache_logf, state_wkv, state_shift, page_table,
           norm_mix, w_in, mu, w0, w2, a0, a2, g2, k_k, k_a, r_k, ln_w, ln_b, b_f, fox_norm,
           w_out, norm_ffn, w_up, w_down, norm_final):
    bp, tp, d = x_prompt.shape
    bs, ts, _ = x_sample.shape
    depth = w_in.shape[0]
    n_heads = b_f.shape[1]
    n_pool, page = cache_k.shape[1], cache_k.shape[2]
    np_tok, ns_tok = bp * tp, bs * ts
    tm_p = _pick_tile(tp, (512, 256, 128))
    tm_s = _pick_tile(ns_tok, (256, 128, 64, 32, 16, 8))
    tq = _pick_tile(tp, (512, 256, 128))
    ts_pad = -(-ts // CHUNK) * CHUNK

    ck = jnp.transpose(cache_k, (0, 1, 3, 4, 2)).reshape(depth, n_pool, MIX_WIDTH, page)
    cv = jnp.transpose(cache_v, (0, 1, 3, 4, 2)).reshape(depth, n_pool, MIX_WIDTH, page)
    clft = jnp.swapaxes(cache_logf, 2, 3)
    nfin = norm_final.reshape(1, -1)

    xp = x_prompt.reshape(np_tok, d)
    xs = x_sample.reshape(ns_tok, d)
    zero_shift = jnp.zeros((bp, 1, RWKV_IN), F32)
    zero_state = jnp.zeros((bp, N_PAIRS, PAIR, PAIR), F32)
    outs_p, outs_s = [], []
    for l in range(depth):
        prm = _layer_params(l, norm_mix, w_in, mu, w0, w2, a0, a2, g2, k_k, k_a, r_k, ln_w, ln_b,
                            b_f, fox_norm, w_out, norm_ffn, w_up, w_down)
        final = l == depth - 1
        proj = lambda x, tm, seq, token_minor: _inproj(x, prm["norm_mix"], prm["wr"], prm["wqkv"], prm["wkvt"],
                                                       prm["wft"], prm["bft"], tm, seq, token_minor)
        heads_last = lambda a, b, t: jnp.transpose(a.reshape(n_heads, b, t), (1, 2, 0))
        zr, kt, vt, qb, kb, vb, lft = proj(xp, tm_p, tp, True)
        k = jnp.transpose(kt.reshape(bp, n_heads, HEAD_DIM, tp), (0, 3, 1, 2))
        v = jnp.transpose(vt.reshape(bp, n_heads, HEAD_DIM, tp), (0, 3, 1, 2))
        zr3 = zr.reshape(bp, tp, RWKV_IN)
        y_r, s_fin = _rwkv(zr3, zero_shift, zero_state, prm, CHUNK)
        c = _cumsum_lanes(lft, bp, tp)
        o_f = _fox_prompt(qb.reshape(bp, tp, -1), kb.reshape(bp, tp, -1), vb.reshape(bp, tp, -1), c,
                          prm["fox_norm"], bp, tp, tq)
        xp = _outmlp(xp, y_r.reshape(np_tok, -1), o_f.reshape(np_tok, -1), prm["w_out"], prm["norm_ffn"],
                     prm["w_up"], prm["w_down"], nfin, tm_p, final)
        outs_p.append((k, v, heads_last(lft, bp, tp), _pairs_to_state(s_fin), zr3[:, -1]))
        zr, k, v, qb, kb, vb, lft = proj(xs, tm_s, ts, False)
        zr3 = zr.reshape(bs, ts, RWKV_IN)
        zr_pad = jnp.pad(zr3, ((0, 0), (0, ts_pad - ts), (0, 0)))
        y_r, s_fin = _rwkv(zr_pad, state_shift[l][:, None, :], _state_to_pairs(state_wkv[l]), prm, ts)
        lfn = jnp.pad(jnp.swapaxes(lft.reshape(n_heads, bs, ts), 0, 1), ((0, 0), (0, 0), (0, page - ts)))
        o_f = _fox_sample(page_table, qb.reshape(bs, ts, -1), k.reshape(bs, ts, -1), v.reshape(bs, ts, -1),
                          lfn, ck, cv, clft, prm["fox_norm"], l)
        xs = _outmlp(xs, y_r[:, :ts].reshape(ns_tok, -1), o_f.reshape(ns_tok, -1), prm["w_out"],
                     prm["norm_ffn"], prm["w_up"], prm["w_down"], nfin, tm_s, final)
        outs_s.append((k.reshape(bs, ts, n_heads, HEAD_DIM), v.reshape(bs, ts, n_heads, HEAD_DIM),
                       heads_last(lft, bs, ts), _pairs_to_state(s_fin), zr3[:, -1]))
    stack = lambda outs, i: jnp.stack([o[i] for o in outs])
    return (xp.reshape(bp, tp, d), xs.reshape(bs, ts, d),
            stack(outs_p, 0), stack(outs_p, 1), stack(outs_p, 2), stack(outs_p, 3), stack(outs_p, 4),
            stack(outs_s, 0), stack(outs_s, 1), stack(outs_s, 2), stack(outs_s, 3), stack(outs_s, 4))
```

```python
import functools

import jax
import jax.numpy as jnp
from jax import lax
from jax.experimental import pallas as pl
from jax.experimental.pallas import tpu as pltpu

F32 = jnp.float32
BF16 = jnp.bfloat16

HEAD_DIM = 64
PAIR = 2 * HEAD_DIM
N_PAIRS = 4
MIX_WIDTH = 512
RWKV_IN = 1792
LORA_WA = 128
NORM_EPS = 1e-6
GN_EPS = 64e-5
CHUNK = 64
SEQS_PER_STEP = 4
NEG_BIG = -1e30
LOG2E = 1.4426950408889634
VMEM_LIMIT = 56 * 1024 * 1024


def _dot(a, b):
    return jnp.dot(a, b, preferred_element_type=F32)


def _dot_nt(a, b):
    return lax.dot_general(a, b, (((1,), (1,)), ((), ())), preferred_element_type=F32)


def _dot_tn(a, b):
    return lax.dot_general(a, b, (((0,), (0,)), ((), ())), preferred_element_type=F32)


def _split2(x):
    hi = x.astype(BF16)
    lo = (x - hi.astype(F32)).astype(BF16)
    return hi, lo


def _split3(x):
    hi = x.astype(BF16)
    r1 = x - hi.astype(F32)
    mid = r1.astype(BF16)
    lo = (r1 - mid.astype(F32)).astype(BF16)
    return hi, mid, lo


def _dot_const_l(c, x):
    hi, mid, lo = _split3(x)
    return _dot(c, hi) + _dot(c, mid) + _dot(c, lo)


def _dot_const_r(x, c):
    hi, mid, lo = _split3(x)
    return _dot(hi, c) + _dot(mid, c) + _dot(lo, c)


def _mm_hi(a, b):
    ah, al = _split2(a)
    bh, bl = _split2(b)
    return _dot(ah, bh) + _dot(ah, bl) + _dot(al, bh)


def _rms(x, w):
    return x * lax.rsqrt(jnp.mean(x * x, axis=-1, keepdims=True) + NORM_EPS) * w


def _log_sigmoid(x):
    return jnp.minimum(x, 0.0) - jnp.log1p(jnp.exp(-jnp.abs(x)))


def _softplus(x):
    return jnp.maximum(x, 0.0) + jnp.log1p(jnp.exp(-jnp.abs(x)))


def _sigmoid(x):
    return 1.0 / (1.0 + jnp.exp(-x))


def _iota2(shape, dim):
    return lax.broadcasted_iota(jnp.int32, shape, dim)


def _seg_ones(n, seg):
    r = _iota2((n, n), 0) // seg
    c = _iota2((n, n), 1) // seg
    return (r == c).astype(BF16)


def _seg_sum(x, ones):
    hi, lo = _split2(x)
    return _dot(hi, ones) + _dot(lo, ones)


def _const_spec(shape):
    nd = len(shape)
    return pl.BlockSpec(shape, lambda *_: (0,) * nd)


def _inproj_kernel(x_ref, nw_ref, wr_ref, wqkv_ref, wkvt_ref, wft_ref, bft_ref,
                   zr_ref, k_ref, v_ref, qb_ref, kb_ref, vb_ref, lft_ref, *, token_minor):
    x = x_ref[...]
    h = _rms(x, nw_ref[...]).astype(BF16)
    zr_ref[...] = _dot(h, wr_ref[...])
    qkv = _dot(h, wqkv_ref[...])
    q = qkv[:, :MIX_WIDTH]
    k = qkv[:, MIX_WIDTH:2 * MIX_WIDTH]
    v = qkv[:, 2 * MIX_WIDTH:]
    if token_minor:
        kvt = _dot_nt(wkvt_ref[...], h)
        k_ref[...] = kvt[:MIX_WIDTH]
        v_ref[...] = kvt[MIX_WIDTH:]
    else:
        k_ref[...] = k
        v_ref[...] = v
    qb_ref[...] = (q * (HEAD_DIM ** -0.5 * LOG2E)).astype(BF16)
    kb_ref[...] = k.astype(BF16)
    vb_ref[...] = v.astype(BF16)
    lft = _log_sigmoid(_dot_nt(wft_ref[...], h) + bft_ref[...])
    lft_ref[...] = lft[:8]


def _inproj(x, nw, wr, wqkv, wkvt, wft, bft, tm, seq_len, token_minor):
    n, d = x.shape
    grid = (n // tm,)
    row = lambda w: pl.BlockSpec((tm, w), lambda i: (i, 0))
    if token_minor:
        nb = seq_len // tm
        kv_shape = jax.ShapeDtypeStruct((n // seq_len, MIX_WIDTH, seq_len), F32)
        kv_spec = pl.BlockSpec((None, MIX_WIDTH, tm), lambda i: (i // nb, 0, i % nb))
    else:
        kv_shape = jax.ShapeDtypeStruct((n, MIX_WIDTH), F32)
        kv_spec = row(MIX_WIDTH)
    out_shape = (
        jax.ShapeDtypeStruct((n, RWKV_IN), F32), kv_shape, kv_shape,
        jax.ShapeDtypeStruct((n, MIX_WIDTH), BF16),
        jax.ShapeDtypeStruct((n, MIX_WIDTH), BF16),
        jax.ShapeDtypeStruct((n, MIX_WIDTH), BF16),
        jax.ShapeDtypeStruct((8, n), F32),
    )
    return pl.pallas_call(
        functools.partial(_inproj_kernel, token_minor=token_minor),
        grid=grid,
        in_specs=[row(d), _const_spec(nw.shape), _const_spec(wr.shape), _const_spec(wqkv.shape),
                  _const_spec(wkvt.shape), _const_spec(wft.shape), _const_spec(bft.shape)],
        out_specs=(row(RWKV_IN), kv_spec, kv_spec, row(MIX_WIDTH), row(MIX_WIDTH),
                   row(MIX_WIDTH), pl.BlockSpec((8, tm), lambda i: (0, i))),
        out_shape=out_shape,
        compiler_params=pltpu.CompilerParams(dimension_semantics=("arbitrary",),
                                             vmem_limit_bytes=VMEM_LIMIT),
        name="inproj",
    )(x, nw, wr, wqkv, wkvt, wft, bft)


def _outmlp_kernel(x_ref, yr_ref, of_ref, wo_ref, nf_ref, wu_ref, wd_ref, nfin_ref, o_ref,
                   *, ff_tile, final):
    wo = wo_ref
    x = x_ref[...] + _dot(yr_ref[...], wo[:MIX_WIDTH, :]) + _dot(of_ref[...], wo[MIX_WIDTH:, :])
    h2 = _rms(x, nf_ref[...]).astype(BF16)
    mlp = None
    d_ff = wu_ref.shape[1]
    for j in range(d_ff // ff_tile):
        u = _dot(h2, wu_ref[:, j * ff_tile:(j + 1) * ff_tile])
        u = jnp.square(jnp.maximum(u, 0.0)).astype(BF16)
        t = _dot(u, wd_ref[j * ff_tile:(j + 1) * ff_tile, :])
        mlp = t if mlp is None else mlp + t
    acc = x + mlp
    if final:
        acc = _rms(acc, nfin_ref[...])
    o_ref[...] = acc


def _outmlp(x, yr, of, wo, nf, wu, wd, nfin, tm, final):
    n, d = x.shape
    grid = (n // tm,)
    row = lambda w: pl.BlockSpec((tm, w), lambda i: (i, 0))
    kern = functools.partial(_outmlp_kernel, ff_tile=1024, final=final)
    return pl.pallas_call(
        kern,
        grid=grid,
        in_specs=[row(d), row(MIX_WIDTH), row(MIX_WIDTH), _const_spec(wo.shape), _const_spec(nf.shape),
                  _const_spec(wu.shape), _const_spec(wd.shape), _const_spec(nfin.shape)],
        out_specs=row(d),
        out_shape=jax.ShapeDtypeStruct((n, d), F32),
        compiler_params=pltpu.CompilerParams(dimension_semantics=("arbitrary",),
                                             vmem_limit_bytes=VMEM_LIMIT),
        name="outmlp",
    )(x, yr, of, wo, nf, wu, wd, nfin)


def _unit_lower_inverse(n_mats, c):
    r = _iota2((c, c), 0)
    col = _iota2((c, c), 1)
    eye = (r == col).astype(F32)
    same8 = (r // 8) == (col // 8)
    n8 = [jnp.where(same8, n, 0.0) for n in n_mats]
    n8_2 = [_mm_hi(n, n) for n in n8]
    ts = [eye + n for n in n8]
    n8_4 = [_mm_hi(n, n) for n in n8_2]
    ts = [t + _mm_hi(t, n) for t, n in zip(ts, n8_2)]
    ts = [t + _mm_hi(t, n) for t, n in zip(ts, n8_4)]
    size = 8
    while size < c // 2:
        same_lo = (r // size) == (col // size)
        same_hi = (r // (2 * size)) == (col // (2 * size))
        off = jnp.logical_and(same_hi, jnp.logical_not(same_lo))
        xs = [_mm_hi(jnp.where(off, n, 0.0), t) for n, t in zip(n_mats, ts)]
        ts = [t + _mm_hi(t, x) for t, x in zip(ts, xs)]
        size *= 2
    return ts


def _rwkv_kernel(z_ref, sh0_ref, s0_ref, mu_ref, w0_ref, a0_ref, kk_ref, ka_ref, rk_ref,
                 lnw_ref, lnb_ref, w2_ref, a2_ref, g2_ref,
                 y_ref, sout_ref, carry_ref, s_ref, *, t_valid):
    c = pl.program_id(1)
    n_chunks = pl.num_programs(1)
    NB, C = z_ref.shape[0], z_ref.shape[1]
    C2 = 2 * C
    R = NB * C

    @pl.when(c == 0)
    def _():
        carry_ref[...] = sh0_ref[...]
        s_ref[...] = s0_ref[...]

    z = z_ref[...].reshape(R, RWKV_IN)
    rows = _iota2((R, 1), 0)
    z_prev = pltpu.roll(z, 1, 0)
    for b in range(NB):
        z_prev = jnp.where(rows == b * C, carry_ref[b], z_prev)
        carry_ref[b] = z[(b + 1) * C - 1:(b + 1) * C, :]
    zs = z + (z_prev - z) * mu_ref[...]
    r_all = zs[:, 0:MIX_WIDTH]
    k_all = zs[:, MIX_WIDTH:2 * MIX_WIDTH]
    v_all = zs[:, 2 * MIX_WIDTH:3 * MIX_WIDTH]
    wa = zs[:, 3 * MIX_WIDTH:3 * MIX_WIDTH + LORA_WA]
    gd = zs[:, 3 * MIX_WIDTH + LORA_WA:]

    w_log = -_softplus(-(w0_ref[...] + _dot(jnp.tanh(wa).astype(BF16), w2_ref[...]))) - 0.5
    logw = -jnp.exp(w_log)
    a_all = _sigmoid(a0_ref[...] + _dot(wa.astype(BF16), a2_ref[...]))
    g_all = _dot(_sigmoid(gd).astype(BF16), g2_ref[...])
    kkr_all = k_all * kk_ref[...]
    kmod_all = k_all * (1.0 + (a_all - 1.0) * ka_ref[...])

    if t_valid < C:
        valid = (rows % C) < t_valid
        logw = jnp.where(valid, logw, 0.0)
        r_all = jnp.where(valid, r_all, 0.0)
        kkr_all = jnp.where(valid, kkr_all, 0.0)
        kmod_all = jnp.where(valid, kmod_all, 0.0)
        v_all = jnp.where(valid, v_all, 0.0)

    rr = _iota2((R, R), 0)
    cc = _iota2((R, R), 1)
    tri_incl = jnp.logical_and(rr // C == cc // C, rr >= cc).astype(BF16)
    l_incl_all = _dot_const_l(tri_incl, logw)
    seg = _seg_ones(PAIR, HEAD_DIM)

    lane = _iota2((1, PAIR), 1)
    head0 = lane < HEAD_DIM
    r2 = _iota2((C2, C2), 0)
    c2 = _iota2((C2, C2), 1)
    same_head = (r2 // C) == (c2 // C)
    strict = jnp.logical_and(same_head, r2 > c2)
    incl = jnp.logical_and(same_head, r2 >= c2)
    sv = _iota2((PAIR, PAIR), 0) // HEAD_DIM
    sk = _iota2((PAIR, PAIR), 1) // HEAD_DIM
    state_mask = sv == sk

    def stack(x):
        return jnp.concatenate([jnp.where(head0, x, 0.0), jnp.where(head0, 0.0, x)], axis=0)

    def unstack(x):
        return x[:C] + x[C:]

    items = [(b, p) for b in range(NB) for p in range(N_PAIRS)]
    rsl = [slice(b * C, (b + 1) * C) for b, _ in items]
    sls = [slice(p * PAIR, (p + 1) * PAIR) for _, p in items]
    cut = lambda x: [x[rs, sl] for rs, sl in zip(rsl, sls)]
    r_ = cut(r_all)
    v_ = cut(v_all)
    km_ = cut(kmod_all)
    kkr_ = cut(kkr_all)
    ss_ = [_seg_sum(x * x, seg) for x in kkr_]
    kk_ = [x * lax.rsqrt(jnp.maximum(s, 1e-24)) for x, s in zip(kkr_, ss_)]
    b_ = [kk * a for kk, a in zip(kk_, cut(a_all))]
    l_incl = cut(l_incl_all)
    l_tot = [l[C - 1:C, :] for l in l_incl]
    e_neg = [jnp.exp(-l) for l in l_incl]
    e_rem = [jnp.exp(lt - l) for lt, l in zip(l_tot, l_incl)]
    a_t = [-kk * jnp.exp(l - lw) for kk, l, lw in zip(kk_, l_incl, cut(logw))]
    r_t = [r * jnp.exp(l) for r, l in zip(r_, l_incl)]
    la = [stack(x).astype(BF16) for x in a_t]
    lr = [stack(x).astype(BF16) for x in r_t]
    rb = [jnp.concatenate([x, x], axis=0).astype(BF16) for x in (b * e for b, e in zip(b_, e_neg))]
    rk = [jnp.concatenate([x, x], axis=0).astype(BF16) for x in (k * e for k, e in zip(km_, e_neg))]
    n_ab = [jnp.where(strict, _dot_nt(x, y), 0.0) for x, y in zip(la, rb)]
    a_ak = [jnp.where(strict, _dot_nt(x, y), 0.0).astype(BF16) for x, y in zip(la, rk)]
    a_rb = [jnp.where(incl, _dot_nt(x, y), 0.0).astype(BF16) for x, y in zip(lr, rb)]
    a_rk = [jnp.where(incl, _dot_nt(x, y), 0.0).astype(BF16) for x, y in zip(lr, rk)]

    s_prev = [s_ref[b, p] for b, p in items]
    s_bf = [s.astype(BF16) for s in s_prev]
    as0 = [_dot_nt(x.astype(BF16), s) for x, s in zip(a_t, s_bf)]
    rs0 = [_dot_nt(x.astype(BF16), s) for x, s in zip(r_t, s_bf)]
    v_st = [stack(x).astype(BF16) for x in v_]
    rhs_u = [stack(x) + _dot(a, v) for x, a, v in zip(as0, a_ak, v_st)]
    y_v = [_dot(a, v) for a, v in zip(a_rk, v_st)]
    t_inv = _unit_lower_inverse(n_ab, C2)
    u_st = [_mm_hi(t, x) for t, x in zip(t_inv, rhs_u)]
    y_pair = [rs + unstack(_dot(a, u.astype(BF16)) + yv) for rs, a, u, yv in zip(rs0, a_rb, u_st, y_v)]
    lhs = [jnp.concatenate([unstack(u), v], axis=0).astype(BF16) for u, v in zip(u_st, v_)]
    rhs = [jnp.concatenate([b * e, k * e], axis=0).astype(BF16) for b, k, e in zip(b_, km_, e_rem)]
    upd = [jnp.where(state_mask, _dot_tn(x, y), 0.0) for x, y in zip(lhs, rhs)]
    for i, (b, p) in enumerate(items):
        s_ref[b, p] = s_prev[i] * jnp.exp(l_tot[i]) + upd[i]

    mean = [_seg_sum(y, seg) * (1.0 / HEAD_DIM) for y in y_pair]
    dev = [y - m for y, m in zip(y_pair, mean)]
    var = [_seg_sum(d * d, seg) * (1.0 / HEAD_DIM) for d in dev]
    bonus = [_seg_sum(r * k * rk_ref[:, sl], seg) * v for r, k, v, sl in zip(r_, km_, v_, sls)]
    g_ = cut(g_all)
    for i, (b, p) in enumerate(items):
        sl = sls[i]
        yn = dev[i] * lax.rsqrt(var[i] + GN_EPS) * lnw_ref[:, sl] + lnb_ref[:, sl]
        y_ref[b, :, sl] = ((yn + bonus[i]) * g_[i]).astype(y_ref.dtype)

    @pl.when(c == n_chunks - 1)
    def _():
        sout_ref[...] = s_ref[...]


def _rwkv(z, sh0, s0, prm, t_valid):
    b, t_pad, _ = z.shape
    n_chunks = t_pad // CHUNK
    consts = [prm[k] for k in ("mu", "w0", "a0", "k_k", "k_a", "r_k", "ln_w", "ln_b", "w2p", "a2p", "g2")]
    kern = functools.partial(_rwkv_kernel, t_valid=t_valid)
    nb = SEQS_PER_STEP if b % SEQS_PER_STEP == 0 else 1
    return pl.pallas_call(
        kern,
        grid=(b // nb, n_chunks),
        in_specs=[pl.BlockSpec((nb, CHUNK, RWKV_IN), lambda i, c: (i, c, 0)),
                  pl.BlockSpec((nb, 1, RWKV_IN), lambda i, c: (i, 0, 0)),
                  pl.BlockSpec((nb, N_PAIRS, PAIR, PAIR), lambda i, c: (i, 0, 0, 0))]
                 + [_const_spec(x.shape) for x in consts],
        out_specs=(pl.BlockSpec((nb, CHUNK, MIX_WIDTH), lambda i, c: (i, c, 0)),
                   pl.BlockSpec((nb, N_PAIRS, PAIR, PAIR), lambda i, c: (i, 0, 0, 0))),
        out_shape=(jax.ShapeDtypeStruct((b, t_pad, MIX_WIDTH), BF16),
                   jax.ShapeDtypeStruct((b, N_PAIRS, PAIR, PAIR), F32)),
        scratch_shapes=[pltpu.VMEM((nb, 1, RWKV_IN), F32), pltpu.VMEM((nb, N_PAIRS, PAIR, PAIR), F32)],
        compiler_params=pltpu.CompilerParams(dimension_semantics=("arbitrary", "arbitrary"),
                                             vmem_limit_bytes=VMEM_LIMIT),
        name="rwkv",
    )(z, sh0, s0, *consts)


def _state_to_pairs(s):
    b = s.shape[0]
    s = s.reshape(b, N_PAIRS, 2, HEAD_DIM, HEAD_DIM)
    z = jnp.zeros_like(s[:, :, 0])
    top = jnp.concatenate([s[:, :, 0], z], axis=-1)
    bot = jnp.concatenate([z, s[:, :, 1]], axis=-1)
    return jnp.concatenate([top, bot], axis=-2)


def _pairs_to_state(sp):
    b = sp.shape[0]
    s0 = sp[:, :, :HEAD_DIM, :HEAD_DIM]
    s1 = sp[:, :, HEAD_DIM:, HEAD_DIM:]
    return jnp.stack([s0, s1], axis=2).reshape(b, 2 * N_PAIRS, HEAD_DIM, HEAD_DIM)


def _cumsum_kernel(lf_ref, c_ref):
    t = lf_ref.shape[1]
    tri = (_iota2((128, 128), 0) <= _iota2((128, 128), 1)).astype(BF16)
    carry = jnp.zeros((8, 1), F32)
    for j in range(t // 128):
        blk = _dot_const_r(lf_ref[:, j * 128:(j + 1) * 128], tri) + carry
        c_ref[:, j * 128:(j + 1) * 128] = blk
        carry = blk[:, 127:128]


def _cumsum_lanes(lft, b, t):
    return pl.pallas_call(
        _cumsum_kernel,
        grid=(b,),
        in_specs=[pl.BlockSpec((8, t), lambda i: (0, i))],
        out_specs=pl.BlockSpec((8, t), lambda i: (0, i)),
        out_shape=jax.ShapeDtypeStruct(lft.shape, F32),
        compiler_params=pltpu.CompilerParams(dimension_semantics=("arbitrary",)),
        name="cumsum",
    )(lft)


def _fox_prompt_kernel(q_ref, k_ref, v_ref, c_ref, fn_ref, o_ref, m_ref, acc_ref, *, tq):
    p = pl.program_id(1)
    i = pl.program_id(2)
    lane = _iota2((1, PAIR), 1)
    head0 = lane < HEAD_DIM
    q = q_ref[0]
    zero = jnp.zeros_like(q)
    qs = (jnp.where(head0, q, zero), jnp.where(head0, zero, q))

    m_ref[...] = jnp.full(m_ref.shape, NEG_BIG, F32)
    acc_ref[...] = jnp.zeros(acc_ref.shape, F32)

    def block(j, masked):
        start = pl.multiple_of(j * tq, tq)
        kb = k_ref[0, pl.ds(start, tq), :]
        vb = v_ref[0, pl.ds(start, tq), :]
        one = jnp.ones_like(vb)
        vs = (jnp.where(head0, vb, one), jnp.where(head0, one, vb))
        for h in range(2):
            ck = c_ref[h:h + 1, pl.ds(start, tq)] * LOG2E
            s = _dot_nt(qs[h], kb) - ck
            if masked:
                s = jnp.where(_iota2((tq, tq), 0) >= _iota2((tq, tq), 1), s, NEG_BIG)
            m = m_ref[h]
            m_new = jnp.maximum(m, jnp.max(s, axis=-1, keepdims=True))
            alpha = jnp.exp2(m - m_new)
            pr = jnp.exp2(s - m_new)
            acc_ref[h] = alpha * acc_ref[h] + _dot(pr.astype(BF16), vs[h])
            m_ref[h] = m_new

    def body(j, carry):
        block(j, False)
        return carry

    lax.fori_loop(0, i, body, 0)
    block(i, True)
    acc0 = acc_ref[0]
    acc1 = acc_ref[1]
    o = jnp.where(head0, acc0 / pltpu.roll(acc0, HEAD_DIM, 1), acc1 / pltpu.roll(acc1, HEAD_DIM, 1))
    o2 = o * o
    ms0 = jnp.sum(jnp.where(head0, o2, 0.0), axis=-1, keepdims=True)
    ms1 = jnp.sum(jnp.where(head0, 0.0, o2), axis=-1, keepdims=True)
    ms = jnp.where(head0, ms0, ms1) * (1.0 / HEAD_DIM)
    o_ref[0] = (o * lax.rsqrt(ms + NORM_EPS) * fn_ref[...]).astype(o_ref.dtype)


def _fox_prompt(qb, kb, vb, c, fn, b, t, tq):
    kern = functools.partial(_fox_prompt_kernel, tq=tq)
    return pl.pallas_call(
        kern,
        grid=(b, N_PAIRS, t // tq),
        in_specs=[pl.BlockSpec((1, tq, PAIR), lambda bi, p, i: (bi, i, p)),
                  pl.BlockSpec((1, t, PAIR), lambda bi, p, i: (bi, 0, p)),
                  pl.BlockSpec((1, t, PAIR), lambda bi, p, i: (bi, 0, p)),
                  pl.BlockSpec((None, 2, t), lambda bi, p, i: (p, 0, bi)),
                  pl.BlockSpec((1, PAIR), lambda bi, p, i: (0, p))],
        out_specs=pl.BlockSpec((1, tq, PAIR), lambda bi, p, i: (bi, i, p)),
        out_shape=jax.ShapeDtypeStruct((b, t, MIX_WIDTH), BF16),
        scratch_shapes=[pltpu.VMEM((2, tq, 1), F32), pltpu.VMEM((2, tq, PAIR), F32)],
        compiler_params=pltpu.CompilerParams(
            dimension_semantics=("arbitrary", "arbitrary", "arbitrary"),
            vmem_limit_bytes=VMEM_LIMIT),
        name="fox_prompt",
    )(qb, kb, vb, c.reshape(N_PAIRS, 2, -1), fn)


PAGES_PER_STEP = 8


def _fox_sample_kernel(pt_ref, q_ref, kn_ref, vn_ref, lfn_ref, *rest, t_new, group):
    kc_refs = rest[:group]
    vc_refs = rest[group:2 * group]
    lfc_refs = rest[2 * group:3 * group]
    fn_ref, o_ref, qbd_ref, m_ref, l_ref, acc_ref, carry_ref = rest[3 * group:]
    j = pl.program_id(1)
    n_steps = pl.num_programs(1)
    n_heads = 2 * N_PAIRS
    n_rows = n_heads * t_new
    page = lfn_ref.shape[2]
    after = (_iota2((page, page), 0) > _iota2((page, page), 1)).astype(BF16)
    lane_head = _iota2((1, MIX_WIDTH), 1) // HEAD_DIM

    def expand(d):
        n = d.shape[1]
        return jnp.concatenate([jnp.broadcast_to(d[h:h + 1, :], (t_new, n)) for h in range(n_heads)], axis=0)

    def update(s, vb, v_token_minor):
        m = m_ref[...]
        m_new = jnp.maximum(m, jnp.max(s, axis=-1, keepdims=True))
        alpha = jnp.exp2(m - m_new)
        pr = jnp.exp2(s - m_new)
        l_ref[...] = alpha * l_ref[...] + jnp.sum(pr, axis=-1, keepdims=True)
        pv = _dot_nt(pr.astype(BF16), vb) if v_token_minor else _dot(pr.astype(BF16), vb)
        acc_ref[...] = alpha * acc_ref[...] + pv
        m_ref[...] = m_new

    @pl.when(j == 0)
    def _():
        q = q_ref[0].astype(F32)
        qbd = jnp.concatenate([jnp.where(lane_head == h, q, 0.0) for h in range(n_heads)], axis=0)
        qbd_ref[...] = qbd.astype(BF16)
        m_ref[...] = jnp.full(m_ref.shape, NEG_BIG, F32)
        l_ref[...] = jnp.zeros(l_ref.shape, F32)
        acc_ref[...] = jnp.zeros(acc_ref.shape, F32)
        pad = jnp.zeros((page - t_new, MIX_WIDTH), F32)
        kn = jnp.concatenate([kn_ref[0], pad], axis=0).astype(BF16)
        vn = jnp.concatenate([vn_ref[0], pad], axis=0).astype(BF16)
        lfn = lfn_ref[0]
        d = expand(_dot_const_r(lfn, after) * LOG2E)
        s = _dot_nt(qbd_ref[...], kn) + d
        key = _iota2((n_rows, page), 1)
        qpos = _iota2((n_rows, page), 0) % t_new
        s = jnp.where(key <= qpos, s, NEG_BIG)
        update(s, vn, False)
        carry_ref[...] = jnp.sum(lfn, axis=-1, keepdims=True)

    carry = carry_ref[...]
    ds = []
    for g in range(group):
        lfc = lfc_refs[g][...]
        ds.append((_dot_const_r(lfc, after) + carry) * LOG2E)
        carry = carry + jnp.sum(lfc, axis=-1, keepdims=True)
    carry_ref[...] = carry
    kt_all = jnp.concatenate([r[...].astype(BF16) for r in kc_refs], axis=1)
    vt_all = jnp.concatenate([r[...].astype(BF16) for r in vc_refs], axis=1)
    s = _dot(qbd_ref[...], kt_all) + expand(jnp.concatenate(ds, axis=1))
    update(s, vt_all, True)

    @pl.when(j == n_steps - 1)
    def _():
        of = acc_ref[...] / l_ref[...]
        o = jnp.zeros((t_new, MIX_WIDTH), F32)
        for h in range(n_heads):
            o = o + jnp.where(lane_head == h, of[h * t_new:(h + 1) * t_new, :], 0.0)
        seg = _seg_ones(PAIR, HEAD_DIM)
        for p in range(N_PAIRS):
            sl = slice(p * PAIR, (p + 1) * PAIR)
            op = o[:, sl]
            ms = _seg_sum(op * op, seg) * (1.0 / HEAD_DIM)
            o_ref[0, :, sl] = (op * lax.rsqrt(ms + NORM_EPS) * fn_ref[:, sl]).astype(o_ref.dtype)


def _fox_sample(page_table, qb, kn, vn, lfn, cache_k, cache_v, cache_lft, fn, layer):
    b, t_new, _ = qb.shape
    n_pages = page_table.shape[1]
    page = cache_lft.shape[3]
    n_rows = 2 * N_PAIRS * t_new
    group = PAGES_PER_STEP if n_pages % PAGES_PER_STEP == 0 else 1
    kern = functools.partial(_fox_sample_kernel, t_new=t_new, group=group)

    def page_idx(g):
        return lambda bi, j, pt: (layer, pt[bi * n_pages + (n_pages - 1 - (j * group + g))], 0, 0)

    tok = lambda bi, j, pt: (bi, 0, 0)
    grid_spec = pltpu.PrefetchScalarGridSpec(
        num_scalar_prefetch=1,
        grid=(b, n_pages // group),
        in_specs=[pl.BlockSpec((1, t_new, MIX_WIDTH), tok), pl.BlockSpec((1, t_new, MIX_WIDTH), tok),
                  pl.BlockSpec((1, t_new, MIX_WIDTH), tok), pl.BlockSpec((1, 8, page), tok)]
                 + [pl.BlockSpec((None, None, MIX_WIDTH, page), page_idx(g)) for g in range(group)]
                 + [pl.BlockSpec((None, None, MIX_WIDTH, page), page_idx(g)) for g in range(group)]
                 + [pl.BlockSpec((None, None, 8, page), page_idx(g)) for g in range(group)]
                 + [pl.BlockSpec((1, MIX_WIDTH), lambda bi, j, pt: (0, 0))],
        out_specs=pl.BlockSpec((1, t_new, MIX_WIDTH), tok),
        scratch_shapes=[pltpu.VMEM((n_rows, MIX_WIDTH), BF16), pltpu.VMEM((n_rows, 1), F32),
                        pltpu.VMEM((n_rows, 1), F32), pltpu.VMEM((n_rows, MIX_WIDTH), F32),
                        pltpu.VMEM((8, 1), F32)],
    )
    return pl.pallas_call(
        kern,
        grid_spec=grid_spec,
        out_shape=jax.ShapeDtypeStruct((b, t_new, MIX_WIDTH), BF16),
        compiler_params=pltpu.CompilerParams(dimension_semantics=("arbitrary", "arbitrary"),
                                             vmem_limit_bytes=VMEM_LIMIT),
        name="fox_sample",
    )(page_table.reshape(-1), qb, kn, vn, lfn, *([cache_k] * group), *([cache_v] * group),
      *([cache_lft] * group), fn)


def _layer_params(l, norm_mix, w_in, mu, w0, w2, a0, a2, g2, k_k, k_a, r_k, ln_w, ln_b, b_f,
                  fox_norm, w_out, norm_ffn, w_up, w_down):
    row = lambda x: x.reshape(1, -1)
    wi = w_in[l]
    n_h = b_f.shape[1]
    wf = wi[:, RWKV_IN + 3 * MIX_WIDTH:]
    zeros64 = jnp.zeros((64, MIX_WIDTH), F32)
    return dict(
        norm_mix=row(norm_mix[l]),
        wr=wi[:, :RWKV_IN].astype(BF16),
        wqkv=wi[:, RWKV_IN:RWKV_IN + 3 * MIX_WIDTH].astype(BF16),
        wkvt=wi[:, RWKV_IN + MIX_WIDTH:RWKV_IN + 3 * MIX_WIDTH].T.astype(BF16),
        wft=jnp.pad(wf.T, ((0, 16 - n_h), (0, 0))).astype(BF16),
        bft=jnp.pad(b_f[l].reshape(-1, 1), ((0, 16 - n_h), (0, 0))),
        mu=row(mu[l]), w0=row(w0[l]), a0=row(a0[l]), k_k=row(k_k[l]), k_a=row(k_a[l]),
        r_k=row(r_k[l]), ln_w=row(ln_w[l]), ln_b=row(ln_b[l]),
        w2p=jnp.concatenate([w2[l], zeros64], axis=0).astype(BF16),
        a2p=jnp.concatenate([zeros64, a2[l]], axis=0).astype(BF16),
        g2=g2[l].astype(BF16),
        fox_norm=row(fox_norm[l]),
        w_out=w_out[l].astype(BF16),
        norm_ffn=row(norm_ffn[l]),
        w_up=w_up[l].astype(BF16),
        w_down=w_down[l].astype(BF16),
    )


def _pick_tile(n, pref):
    for t in pref:
        if n % t == 0:
            return t
    return n


def kernel(x_prompt, x_sample, cache_k, cache_v, cache_logf, state_wkv, state_shift, page_table,
           norm_mix, w_in, mu, w0, w2, a0, a2, g2, k_k, k_a, r_k, ln_w, ln_b, b_f, fox_norm,
           w_out, norm_ffn, w_up, w_down, norm_final):
    bp, tp, d = x_prompt.shape
    bs, ts, _ = x_sample.shape
    depth = w_in.shape[0]
    n_heads = b_f.shape[1]
    n_pool, page = cache_k.shape[1], cache_k.shape[2]
    np_tok, ns_tok = bp * tp, bs * ts
    tm_p = _pick_tile(tp, (512, 256, 128))
    tm_s = _pick_tile(ns_tok, (256, 128, 64, 32, 16, 8))
    tq = _pick_tile(tp, (512, 256, 128))
    ts_pad = -(-ts // CHUNK) * CHUNK

    ck = jnp.transpose(cache_k, (0, 1, 3, 4, 2)).reshape(depth, n_pool, MIX_WIDTH, page)
    cv = jnp.transpose(cache_v, (0, 1, 3, 4, 2)).reshape(depth, n_pool, MIX_WIDTH, page)
    clft = jnp.swapaxes(cache_logf, 2, 3)
    nfin = norm_final.reshape(1, -1)

    xp = x_prompt.reshape(np_tok, d)
    xs = x_sample.reshape(ns_tok, d)
    zero_shift = jnp.zeros((bp, 1, RWKV_IN), F32)
    zero_state = jnp.zeros((bp, N_PAIRS, PAIR, PAIR), F32)
    outs_p, outs_s = [], []
    for l in range(depth):
        prm = _layer_params(l, norm_mix, w_in, mu, w0, w2, a0, a2, g2, k_k, k_a, r_k, ln_w, ln_b,
                            b_f, fox_norm, w_out, norm_ffn, w_up, w_down)
        final = l == depth - 1
        proj = lambda x, tm, seq, token_minor: _inproj(x, prm["norm_mix"], prm["wr"], prm["wqkv"], prm["wkvt"],
                                                       prm["wft"], prm["bft"], tm, seq, token_minor)
        heads_last = lambda a, b, t: jnp.transpose(a.reshape(n_heads, b, t), (1, 2, 0))
        zr, kt, vt, qb, kb, vb, lft = proj(xp, tm_p, tp, True)
        k = jnp.transpose(kt.reshape(bp, n_heads, HEAD_DIM, tp), (0, 3, 1, 2))
        v = jnp.transpose(vt.reshape(bp, n_heads, HEAD_DIM, tp), (0, 3, 1, 2))
        zr3 = zr.reshape(bp, tp, RWKV_IN)
        y_r, s_fin = _rwkv(zr3, zero_shift, zero_state, prm, CHUNK)
        c = _cumsum_lanes(lft, bp, tp)
        o_f = _fox_prompt(qb.reshape(bp, tp, -1), kb.reshape(bp, tp, -1), vb.reshape(bp, tp, -1), c,
                          prm["fox_norm"], bp, tp, tq)
        xp = _outmlp(xp, y_r.reshape(np_tok, -1), o_f.reshape(np_tok, -1), prm["w_out"], prm["norm_ffn"],
                     prm["w_up"], prm["w_down"], nfin, tm_p, final)
        outs_p.append((k, v, heads_last(lft, bp, tp), _pairs_to_state(s_fin), zr3[:, -1]))
        zr, k, v, qb, kb, vb, lft = proj(xs, tm_s, ts, False)
        zr3 = zr.reshape(bs, ts, RWKV_IN)
        zr_pad = jnp.pad(zr3, ((0, 0), (0, ts_pad - ts), (0, 0)))
        y_r, s_fin = _rwkv(zr_pad, state_shift[l][:, None, :], _state_to_pairs(state_wkv[l]), prm, ts)
        lfn = jnp.pad(jnp.swapaxes(lft.reshape(n_heads, bs, ts), 0, 1), ((0, 0), (0, 0), (0, page - ts)))
        o_f = _fox_sample(page_table, qb.reshape(bs, ts, -1), k.reshape(bs, ts, -1), v.reshape(bs, ts, -1),
                          lfn, ck, cv, clft, prm["fox_norm"], l)
        xs = _outmlp(xs, y_r[:, :ts].reshape(ns_tok, -1), o_f.reshape(ns_tok, -1), prm["w_out"],
                     prm["norm_ffn"], prm["w_up"], prm["w_down"], nfin, tm_s, final)
        outs_s.append((k.reshape(bs, ts, n_heads, HEAD_DIM), v.reshape(bs, ts, n_heads, HEAD_DIM),
                       heads_last(lft, bs, ts), _pairs_to_state(s_fin), zr3[:, -1]))
    stack = lambda outs, i: jnp.stack([o[i] for o in outs])
    return (xp.reshape(bp, tp, d), xs.reshape(bs, ts, d),
            stack(outs_p, 0), stack(outs_p, 1), stack(outs_p, 2), stack(outs_p, 3), stack(outs_p, 4),
            stack(outs_s, 0), stack(outs_s, 1), stack(outs_s, 2), stack(outs_s, 3), stack(outs_s, 4))
```
